```python
import math
import jax, jax.numpy as jnp
from jax import lax
import numpy as np

D_MODEL = 1024
BATCH = 2
SEQ = 8192
DEPTH = 1
DEC_BATCH = 16
DEC_SEQ = 32
PAST_LEN = 2048

CHUNK = 64
Q_BLOCK = 128
SB_HEADS = 8
SB_HEAD_DIM = 64
SB_WIDTH = SB_HEADS * SB_HEAD_DIM
MLA_HEADS = 8
MLA_NOPE_DIM = 64
MLA_ROPE_DIM = 32
MLA_V_DIM = 64
MLA_Q_LORA = 384
MLA_KV_LORA = 256
MLA_WIDTH = MLA_HEADS * MLA_V_DIM
ROPE_THETA = 10000.0
PEER_HEADS = 8
PEER_N_KEYS = 128
PEER_N_EXPERTS = PEER_N_KEYS * PEER_N_KEYS
PEER_TOPK = 16
PEER_KEY_DIM = 256
PEER_HALF = PEER_KEY_DIM // 2
PEER_BLOCK = 128
ALPHA = (2.0 * DEPTH) ** 0.25
BETA = (8.0 * DEPTH) ** -0.25
EPS = 1e-5
NEG_INF = -1e30
IN_SIZES = (SB_WIDTH, SB_WIDTH, SB_WIDTH, MLA_Q_LORA, MLA_KV_LORA, MLA_ROPE_DIM, D_MODEL, D_MODEL)
IN_SPLITS = tuple(int(s) for s in np.cumsum(IN_SIZES)[:-1])
IN_WIDTH = int(sum(IN_SIZES))

kernel_name = "hybrid_stickbreak_mla_peer_stream_step"


def layer_norm(x, g, b):
    xf = x.astype(jnp.float32)
    mu = jnp.mean(xf, axis=-1, keepdims=True)
    var = jnp.mean(jnp.square(xf - mu), axis=-1, keepdims=True)
    return ((xf - mu) * lax.rsqrt(var + EPS) * g.astype(jnp.float32) + b.astype(jnp.float32)).astype(x.dtype)


def rms_norm(x, g):
    xf = x.astype(jnp.float32)
    ms = jnp.mean(jnp.square(xf), axis=-1, keepdims=True)
    return (xf * lax.rsqrt(ms + EPS) * g.astype(jnp.float32)).astype(x.dtype)


def rope(x, pos):
    half = x.shape[-1] // 2
    inv = ROPE_THETA ** (-jnp.arange(half, dtype=jnp.float32) / half)
    ang = pos.astype(jnp.float32)[:, None] * inv[None, :]
    shape = (ang.shape[0],) + (1,) * (x.ndim - 3) + (half,)
    cos = jnp.cos(ang).reshape(shape).astype(x.dtype)
    sin = jnp.sin(ang).reshape(shape).astype(x.dtype)
    x1, x2 = x[..., :half], x[..., half:]
    return jnp.concatenate([x1 * cos - x2 * sin, x1 * sin + x2 * cos], axis=-1)


def sweep_queries(attend, q_arrays, q_pos):
    tq = q_pos.shape[0]
    if tq <= Q_BLOCK or tq % Q_BLOCK != 0:
        return attend(*q_arrays, q_pos)
    nb = tq // Q_BLOCK

    def split(a):
        return jnp.swapaxes(a.reshape((a.shape[0], nb, Q_BLOCK) + a.shape[2:]), 0, 1)

    blocks = tuple(split(a) for a in q_arrays) + (q_pos.reshape(nb, Q_BLOCK),)
    out = lax.map(lambda args: attend(*args), blocks)
    out = jnp.swapaxes(out, 0, 1)
    return out.reshape((out.shape[0], tq) + out.shape[3:])


def stick_breaking_attend(q, k, v, q_pos, k_pos):
    z = jnp.einsum('bqhd,bkhd->bhqk', q, k, preferred_element_type=jnp.float32) * (SB_HEAD_DIM ** -0.5)
    causal = k_pos[None, :] < q_pos[:, None]
    log_beta = jax.nn.log_sigmoid(z)
    log_one_minus = jnp.where(causal, jax.nn.log_sigmoid(-z), 0.0)
    suffix = lax.cumsum(log_one_minus, axis=3, reverse=True) - log_one_minus
    a = jnp.where(causal, jnp.exp(log_beta + suffix), 0.0)
    return jnp.einsum('bhqk,bkhd->bqhd', a.astype(v.dtype), v)


def mla_attend(q_nope, q_pe, k_nope, k_pe, v, q_pos, k_pos):
    s = (jnp.einsum('bqhd,bkhd->bhqk', q_nope, k_nope, preferred_element_type=jnp.float32)
         + jnp.einsum('bqhr,bkr->bhqk', q_pe, k_pe, preferred_element_type=jnp.float32))
    s = s * ((MLA_NOPE_DIM + MLA_ROPE_DIM) ** -0.5)
    mask = (k_pos // CHUNK)[None, :] <= (q_pos // CHUNK)[:, None]
    p = jax.nn.softmax(jnp.where(mask, s, NEG_INF), axis=-1)
    return jnp.einsum('bhqk,bkhd->bqhd', p.astype(v.dtype), v)


def peer(h, peer_wq, peer_sub_keys, peer_u, peer_v):
    n = h.shape[0]
    nb = -(-n // PEER_BLOCK)
    hp = jnp.pad(h, ((0, nb * PEER_BLOCK - n), (0, 0))).reshape(nb, PEER_BLOCK, D_MODEL)

    def block(xb):
        q = (xb @ peer_wq).reshape(PEER_BLOCK, PEER_HEADS, 2, PEER_HALF)
        s = jnp.einsum('thcd,hckd->thck', q, peer_sub_keys, preferred_element_type=jnp.float32)
        s1, i1 = lax.top_k(s[:, :, 0], PEER_TOPK)
        s2, i2 = lax.top_k(s[:, :, 1], PEER_TOPK)
        cand_s = (s1[..., :, None] + s2[..., None, :]).reshape(PEER_BLOCK, PEER_HEADS, PEER_TOPK * PEER_TOPK)
        cand_i = (i1[..., :, None] * PEER_N_KEYS + i2[..., None, :]).reshape(PEER_BLOCK, PEER_HEADS, PEER_TOPK * PEER_TOPK)
        top_s, top_pos = lax.top_k(cand_s, PEER_TOPK)
        experts = jnp.take_along_axis(cand_i, top_pos, axis=-1)
        g = jax.nn.softmax(top_s, axis=-1).astype(xb.dtype)
        u = peer_u[experts]
        act = jax.nn.gelu(jnp.einsum('thkd,td->thk', u, xb), approximate=False)
        v = peer_v[experts]
        return jnp.einsum('thk,thkd->td', g * act, v)

    out = lax.map(block, hp).reshape(nb * PEER_BLOCK, D_MODEL)
    return out[:n]


def layer(x, q_pos, past, w_in, q_norm_g, w_uq, kv_norm_g, w_uk, w_uv, w_oa, w_ob, w_out,
          ln1_g, ln1_b, peer_wq, peer_sub_keys, peer_u, peer_v, ln2_g, ln2_b):
    b, t, _ = x.shape
    proj = x @ w_in
    sb_q, sb_k, sb_v, c_q, c_kv, k_r, g_a, g_b = jnp.split(proj, IN_SPLITS, axis=-1)
    sb_q = sb_q.reshape(b, t, SB_HEADS, SB_HEAD_DIM)
    sb_k = sb_k.reshape(b, t, SB_HEADS, SB_HEAD_DIM)
    sb_v = sb_v.reshape(b, t, SB_HEADS, SB_HEAD_DIM)
    q = (rms_norm(c_q, q_norm_g) @ w_uq).reshape(b, t, MLA_HEADS, MLA_NOPE_DIM + MLA_ROPE_DIM)
    q_nope = q[..., :MLA_NOPE_DIM]
    q_pe = rope(q[..., MLA_NOPE_DIM:], q_pos)
    ckv = rms_norm(c_kv, kv_norm_g)
    kpe = rope(k_r, q_pos)
    new_state = (sb_k, sb_v, ckv, kpe)
    if past is None:
        k_sb, v_sb, ckv_all, kpe_all = new_state
        k_pos = q_pos
    else:
        p_k, p_v, p_ckv, p_kpe = past
        k_sb = jnp.concatenate([p_k, sb_k], axis=1)
        v_sb = jnp.concatenate([p_v, sb_v], axis=1)
        ckv_all = jnp.concatenate([p_ckv, ckv], axis=1)
        kpe_all = jnp.concatenate([p_kpe, kpe], axis=1)
        k_pos = jnp.concatenate([jnp.arange(p_k.shape[1], dtype=jnp.int32), q_pos])
    tk = ckv_all.shape[1]
    k_nope = (ckv_all @ w_uk).reshape(b, tk, MLA_HEADS, MLA_NOPE_DIM)
    v_mla = (ckv_all @ w_uv).reshape(b, tk, MLA_HEADS, MLA_V_DIM)

    o_a = sweep_queries(lambda qb, pb: stick_breaking_attend(qb, k_sb, v_sb, pb, k_pos), (sb_q,), q_pos)
    o_b = sweep_queries(lambda qn, qe, pb: mla_attend(qn, qe, k_nope, kpe_all, v_mla, pb, k_pos),
                        (q_nope, q_pe), q_pos)
    merged = (jax.nn.sigmoid(g_a) * (o_a.reshape(b, t, SB_WIDTH) @ w_oa)
              + jax.nn.sigmoid(g_b) * (o_b.reshape(b, t, MLA_WIDTH) @ w_ob))
    h = layer_norm(ALPHA * x + merged @ w_out, ln1_g, ln1_b)
    ff = peer(h.reshape(b * t, D_MODEL), peer_wq, peer_sub_keys, peer_u, peer_v).reshape(b, t, D_MODEL)
    y = layer_norm(ALPHA * h + ff, ln2_g, ln2_b)
    return y, new_state


def setup_inputs(seed: int = 0) -> dict:
    key = jax.random.key(seed)
    ks = jax.random.split(key, 24)
    nrm = jax.random.normal
    f32 = jnp.float32
    w_in = jnp.concatenate([
        nrm(ks[0], (DEPTH, D_MODEL, 2 * SB_WIDTH), f32) * D_MODEL ** -0.5,
        nrm(ks[1], (DEPTH, D_MODEL, SB_WIDTH), f32) * (D_MODEL ** -0.5 * BETA),
        nrm(ks[2], (DEPTH, D_MODEL, IN_WIDTH - 3 * SB_WIDTH), f32) * D_MODEL ** -0.5,
    ], axis=-1)
    return {
        "x_prompt": nrm(ks[3], (BATCH, SEQ, D_MODEL), f32),
        "x_sample": nrm(ks[4], (DEC_BATCH, DEC_SEQ, D_MODEL), f32),
        "cache_sb_k": nrm(ks[5], (DEPTH, DEC_BATCH, PAST_LEN, SB_HEADS, SB_HEAD_DIM), f32),
        "cache_sb_v": nrm(ks[6], (DEPTH, DEC_BATCH, PAST_LEN, SB_HEADS, SB_HEAD_DIM), f32) * BETA,
        "cache_mla_ckv": nrm(ks[7], (DEPTH, DEC_BATCH, PAST_LEN, MLA_KV_LORA), f32),
        "cache_mla_kpe": nrm(ks[8], (DEPTH, DEC_BATCH, PAST_LEN, MLA_ROPE_DIM), f32),
        "w_in": w_in,
        "q_norm_g": 1.0 + 0.01 * nrm(ks[9], (DEPTH, MLA_Q_LORA), f32),
        "w_uq": nrm(ks[10], (DEPTH, MLA_Q_LORA, MLA_HEADS * (MLA_NOPE_DIM + MLA_ROPE_DIM)), f32) * MLA_Q_LORA ** -0.5,
        "kv_norm_g": 1.0 + 0.01 * nrm(ks[11], (DEPTH, MLA_KV_LORA), f32),
        "w_uk": nrm(ks[12], (DEPTH, MLA_KV_LORA, MLA_HEADS * MLA_NOPE_DIM), f32) * MLA_KV_LORA ** -0.5,
        "w_uv": nrm(ks[13], (DEPTH, MLA_KV_LORA, MLA_HEADS * MLA_V_DIM), f32) * (MLA_KV_LORA ** -0.5 * BETA),
        "w_oa": nrm(ks[14], (DEPTH, SB_WIDTH, D_MODEL), f32) * (SB_WIDTH ** -0.5 * BETA),
        "w_ob": nrm(ks[15], (DEPTH, MLA_WIDTH, D_MODEL), f32) * (MLA_WIDTH ** -0.5 * BETA),
        "w_out": nrm(ks[16], (DEPTH, D_MODEL, D_MODEL), f32) * (D_MODEL ** -0.5 * BETA),
        "ln1_g": 1.0 + 0.01 * nrm(ks[17], (DEPTH, D_MODEL), f32),
        "ln1_b": 0.01 * nrm(ks[18], (DEPTH, D_MODEL), f32),
        "peer_wq": nrm(ks[19], (DEPTH, D_MODEL, PEER_HEADS * PEER_KEY_DIM), f32) * D_MODEL ** -0.5,
        "peer_sub_keys": nrm(ks[20], (DEPTH, PEER_HEADS, 2, PEER_N_KEYS, PEER_HALF), f32) * PEER_HALF ** -0.5,
        "peer_u": nrm(ks[21], (DEPTH, PEER_N_EXPERTS, D_MODEL), f32) * D_MODEL ** -0.5,
        "peer_v": nrm(ks[22], (DEPTH, PEER_N_EXPERTS, D_MODEL), f32) * BETA,
        "ln2_g": 1.0 + 0.01 * nrm(ks[23], (DEPTH, D_MODEL), f32),
        "ln2_b": 0.01 * nrm(jax.random.fold_in(ks[23], 1), (DEPTH, D_MODEL), f32),
    }


def reference(x_prompt, x_sample, cache_sb_k, cache_sb_v, cache_mla_ckv, cache_mla_kpe,
              w_in, q_norm_g, w_uq, kv_norm_g, w_uk, w_uv, w_oa, w_ob, w_out, ln1_g, ln1_b,
              peer_wq, peer_sub_keys, peer_u, peer_v, ln2_g, ln2_b):
    past_len = cache_sb_k.shape[2]
    pos_prompt = jnp.arange(x_prompt.shape[1], dtype=jnp.int32)
    pos_sample = past_len + jnp.arange(x_sample.shape[1], dtype=jnp.int32)
    y_p, y_s = x_prompt, x_sample
    states_p, states_s = [], []
    for l in range(DEPTH):
        w = (w_in[l], q_norm_g[l], w_uq[l], kv_norm_g[l], w_uk[l], w_uv[l], w_oa[l], w_ob[l], w_out[l],
             ln1_g[l], ln1_b[l], peer_wq[l], peer_sub_keys[l], peer_u[l], peer_v[l], ln2_g[l], ln2_b[l])
        y_p, st_p = layer(y_p, pos_prompt, None, *w)
        past = (cache_sb_k[l], cache_sb_v[l], cache_mla_ckv[l], cache_mla_kpe[l])
        y_s, st_s = layer(y_s, pos_sample, past, *w)
        states_p.append(st_p)
        states_s.append(st_s)
    sb_k_p = jnp.stack([s[0] for s in states_p])
    sb_v_p = jnp.stack([s[1] for s in states_p])
    ckv_p = jnp.stack([s[2] for s in states_p])
    kpe_p = jnp.stack([s[3] for s in states_p])
    sb_k_s = jnp.stack([s[0] for s in states_s])
    sb_v_s = jnp.stack([s[1] for s in states_s])
    ckv_s = jnp.stack([s[2] for s in states_s])
    kpe_s = jnp.stack([s[3] for s in states_s])
    return (y_p, y_s, sb_k_p, sb_v_p, ckv_p, kpe_p, sb_k_s, sb_v_s, ckv_s, kpe_s)
```

```python
import functools

import numpy as np
import jax
import jax.numpy as jnp
from jax import lax
from jax.experimental import pallas as pl
from jax.experimental.pallas import tpu as pltpu

F32 = jnp.float32
BF16 = jnp.bfloat16

D_MODEL = 1024
CHUNK = 64
SB_HEADS = 8
SB_HEAD_DIM = 64
SB_WIDTH = SB_HEADS * SB_HEAD_DIM
MLA_HEADS = 8
MLA_NOPE_DIM = 64
MLA_ROPE_DIM = 32
MLA_V_DIM = 64
MLA_Q_LORA = 384
MLA_KV_LORA = 256
MLA_WIDTH = MLA_HEADS * MLA_V_DIM
ROPE_THETA = 10000.0
PEER_HEADS = 8
PEER_N_KEYS = 128
PEER_TOPK = 16
PEER_HALF = 128
EPS = 1e-5
NEG_INF = -1e30

LANES = 128
HEAD_PAD = LANES
ROPE_HALF = MLA_ROPE_DIM // 2
VMEM_LIMIT = 56 * 2**20


def _cparams(sem):
    return pltpu.CompilerParams(dimension_semantics=sem, vmem_limit_bytes=VMEM_LIMIT)


def _full(shape):
    nd = len(shape)
    return pl.BlockSpec(shape, lambda *_: (0,) * nd)


def _rms(x, g):
    ms = jnp.mean(x * x, axis=-1, keepdims=True)
    return x * lax.rsqrt(ms + EPS) * g


def _layer_norm(x, g, b):
    mu = jnp.mean(x, axis=-1, keepdims=True)
    xc = x - mu
    var = jnp.mean(xc * xc, axis=-1, keepdims=True)
    return xc * lax.rsqrt(var + EPS) * g + b


def _rope_lanes(x, c, s1, s2):
    return x * c + pltpu.roll(x, LANES - ROPE_HALF, 1) * s1 + pltpu.roll(x, ROPE_HALF, 1) * s2


W_ALL = 3 * SB_WIDTH + MLA_Q_LORA + MLA_KV_LORA + LANES


def _proj_kernel(x_ref, cq_ref, ck_ref, s1_ref, s2_ref, wall_ref, qg_ref, kvg_ref, wuq_ref, wuk_ref, wuv_ref,
                 sbq_o, sbk_o, sbv_o, sbkb_o, sbvb_o, ckv_o, kpe_o, qcat_o, kcat_o, vm_o):
    xb = x_ref[...].astype(BF16)
    proj = jnp.dot(xb, wall_ref[...], preferred_element_type=F32)
    sbq_o[...] = (proj[:, 0:SB_WIDTH] * (SB_HEAD_DIM ** -0.5)).astype(BF16)
    k = proj[:, SB_WIDTH:2 * SB_WIDTH]
    v = proj[:, 2 * SB_WIDTH:3 * SB_WIDTH]
    sbk_o[...] = k
    sbv_o[...] = v
    sbkb_o[...] = k.astype(BF16)
    sbvb_o[...] = v.astype(BF16)
    o = 3 * SB_WIDTH
    cqn = _rms(proj[:, o:o + MLA_Q_LORA], qg_ref[...])
    q = jnp.dot(cqn.astype(BF16), wuq_ref[...], preferred_element_type=F32)
    o += MLA_Q_LORA
    ckv = _rms(proj[:, o:o + MLA_KV_LORA], kvg_ref[...])
    ckv_o[...] = ckv
    o += MLA_KV_LORA
    cb = ckv.astype(BF16)
    kn = jnp.dot(cb, wuk_ref[...], preferred_element_type=F32)
    vm_o[...] = jnp.dot(cb, wuv_ref[...], preferred_element_type=F32).astype(BF16)
    s1 = s1_ref[...]
    s2 = s2_ref[...]
    kp = _rope_lanes(proj[:, o:o + LANES], ck_ref[...], s1, s2)
    kpe_o[...] = kp[:, MLA_NOPE_DIM:MLA_NOPE_DIM + MLA_ROPE_DIM]
    cq = cq_ref[...]
    for h in range(MLA_HEADS):
        sl = slice(h * HEAD_PAD, (h + 1) * HEAD_PAD)
        qcat_o[:, sl] = _rope_lanes(q[:, sl], cq, s1, s2).astype(BF16)
        kcat_o[:, sl] = (kn[:, sl] + kp).astype(BF16)


def _proj(x, tabs, w, tm):
    n = x.shape[0]
    rt = tabs[0].shape[0] // tm
    tok = lambda width: pl.BlockSpec((tm, width), lambda i: (i, 0))
    tab = pl.BlockSpec((tm, LANES), lambda i: (i % rt, 0))
    outs = [(SB_WIDTH, BF16), (SB_WIDTH, F32), (SB_WIDTH, F32), (SB_WIDTH, BF16), (SB_WIDTH, BF16),
            (MLA_KV_LORA, F32), (MLA_ROPE_DIM, F32), (MLA_HEADS * HEAD_PAD, BF16), (MLA_HEADS * HEAD_PAD, BF16),
            (MLA_WIDTH, BF16)]
    weights = (w["w_all"], w["q_g"], w["kv_g"], w["w_uq"], w["w_uk"], w["w_uv"])
    return pl.pallas_call(
        _proj_kernel,
        grid=(n // tm,),
        in_specs=[tok(D_MODEL), tab, tab, tab, tab] + [_full(a.shape) for a in weights],
        out_specs=[tok(wd) for wd, _ in outs],
        out_shape=[jax.ShapeDtypeStruct((n, wd), dt) for wd, dt in outs],
        compiler_params=_cparams(("parallel",)),
        name="proj",
    )(x, *tabs, *weights)


def _cache_kernel(ck_ref, cv_ref, ckv_ref, kpe_ref, wuk_ref, wuv_ref, place_ref, sbkb_o, sbvb_o, kcat_o, vm_o):
    sbkb_o[...] = ck_ref[...].astype(BF16)
    sbvb_o[...] = cv_ref[...].astype(BF16)
    cb = ckv_ref[...].astype(BF16)
    kn = jnp.dot(cb, wuk_ref[...], preferred_element_type=F32)
    vm_o[...] = jnp.dot(cb, wuv_ref[...], preferred_element_type=F32).astype(BF16)
    kp = jnp.dot(kpe_ref[...].astype(BF16), place_ref[...], preferred_element_type=F32)
    for h in range(MLA_HEADS):
        sl = slice(h * HEAD_PAD, (h + 1) * HEAD_PAD)
        kcat_o[:, sl] = (kn[:, sl] + kp).astype(BF16)


def _cache_prep(ck, cv, ckv, kpe, w, tm):
    n = ck.shape[0]
    tok = lambda width: pl.BlockSpec((tm, width), lambda i: (i, 0))
    outs = [(SB_WIDTH, BF16), (SB_WIDTH, BF16), (MLA_HEADS * HEAD_PAD, BF16), (MLA_WIDTH, BF16)]
    weights = (w["w_uk"], w["w_uv"], w["place"])
    return pl.pallas_call(
        _cache_kernel,
        grid=(n // tm,),
        in_specs=[tok(SB_WIDTH), tok(SB_WIDTH), tok(MLA_KV_LORA), tok(MLA_ROPE_DIM)]
        + [_full(a.shape) for a in weights],
        out_specs=[tok(wd) for wd, _ in outs],
        out_shape=[jax.ShapeDtypeStruct((n, wd), dt) for wd, dt in outs],
        compiler_params=_cparams(("parallel",)),
        name="cache_prep",
    )(ck, cv, ckv, kpe, *weights)


def _tile_lanes(x, n):
    return x if n == 1 else jnp.concatenate([x] * n, axis=1)


def _sb_kernel(q_ref, k_ref, v_ref, tri_ref, o_ref, acc_ref, c_ref, *, tq, tk, nk, q_off):
    qi = pl.program_id(2)
    q = q_ref[0]
    lane = lax.broadcasted_iota(jnp.int32, (tq, LANES), 1)
    zero = jnp.zeros_like(q)
    qs = (jnp.where(lane < SB_HEAD_DIM, q, zero), jnp.where(lane >= SB_HEAD_DIM, q, zero))
    q_lo = q_off + qi * tq
    nkb = jnp.minimum(nk, (q_lo + tq - 1 + tk - 1) // tk)
    acc_ref[...] = jnp.zeros_like(acc_ref)
    c_ref[...] = jnp.zeros_like(c_ref)
    q_pos = lax.broadcasted_iota(jnp.int32, (tq, tk), 0) + q_lo
    col = lax.broadcasted_iota(jnp.int32, (tq, tk), 1)
    tri = tri_ref[...]

    def body(j, carry):
        k0 = pl.multiple_of((nkb - 1 - j) * tk, tk)
        k = k_ref[0, pl.ds(k0, tk), :]
        v = v_ref[0, pl.ds(k0, tk), :]
        causal = (col + k0) < q_pos
        for h in range(2):
            z = lax.dot_general(qs[h], k, (((1,), (1,)), ((), ())), preferred_element_type=F32)
            sp = jnp.log1p(jnp.exp(-jnp.abs(z)))
            lb = jnp.minimum(z, 0.0) - sp
            lom = jnp.where(causal, lb - z, 0.0)
            hi = lom.astype(BF16)
            lo = (lom - hi.astype(F32)).astype(BF16)
            se = (jnp.dot(hi, tri, preferred_element_type=F32) + jnp.dot(lo, tri, preferred_element_type=F32))
            c = c_ref[h]
            a = jnp.where(causal, jnp.exp(lb + se[:, :tk] + _tile_lanes(c, tk // LANES)), 0.0)
            acc_ref[h] += jnp.dot(a.astype(BF16), v, preferred_element_type=F32)
            c_ref[h] = c + se[:, tk:]
        return carry

    lax.fori_loop(0, nkb, body, 0)
    o_ref[0] = jnp.where(lane < SB_HEAD_DIM, acc_ref[0], acc_ref[1]).astype(BF16)


def _suffix_matrix(tk):
    j = np.arange(tk)[:, None]
    s = np.arange(tk)[None, :]
    tri = (j > s).astype(np.float32)
    return jnp.asarray(np.concatenate([tri, np.ones((tk, LANES), np.float32)], axis=1), dtype=BF16)


def _sb_attention(q, k, v, q_off, tq, tk):
    b, t_q, _ = q.shape
    t_k = k.shape[1]
    nk = t_k // tk
    kern = functools.partial(_sb_kernel, tq=tq, tk=tk, nk=nk, q_off=q_off)
    kv_spec = pl.BlockSpec((1, t_k, LANES), lambda bi, p, qi: (bi, 0, p))
    return pl.pallas_call(
        kern,
        grid=(b, SB_WIDTH // LANES, t_q // tq),
        in_specs=[pl.BlockSpec((1, tq, LANES), lambda bi, p, qi: (bi, qi, p)), kv_spec, kv_spec,
                  _full((tk, tk + LANES))],
        out_specs=pl.BlockSpec((1, tq, LANES), lambda bi, p, qi: (bi, qi, p)),
        out_shape=jax.ShapeDtypeStruct((b, t_q, SB_WIDTH), BF16),
        scratch_shapes=[pltpu.VMEM((2, tq, LANES), F32), pltpu.VMEM((2, tq, LANES), F32)],
        compiler_params=_cparams(("parallel", "parallel", "arbitrary")),
        name="sb_attention",
    )(q, k, v, _suffix_matrix(tk))


def _mla_kernel(qa_ref, qb_ref, ka_ref, kb_ref, v_ref, o_ref, acc_ref, m_ref, l_ref, *, tq, tk, nk, q_off, tk_valid):
    qi = pl.program_id(2)
    q_lo = q_off + qi * tq
    k_hi = ((q_lo + tq - 1) // CHUNK + 1) * CHUNK
    nkb = jnp.minimum(nk, (k_hi + tk - 1) // tk)
    acc_ref[...] = jnp.zeros_like(acc_ref)
    l_ref[...] = jnp.zeros_like(l_ref)
    m_ref[...] = jnp.full_like(m_ref, NEG_INF)
    q_chunk = (lax.broadcasted_iota(jnp.int32, (tq, tk), 0) + q_lo) // CHUNK
    col = lax.broadcasted_iota(jnp.int32, (tq, tk), 1)
    scale = (MLA_NOPE_DIM + MLA_ROPE_DIM) ** -0.5
    qs = (qa_ref[0], qb_ref[0])
    k_refs = (ka_ref, kb_ref)

    def body(kb, carry):
        k0 = pl.multiple_of(kb * tk, tk)
        v = v_ref[0, pl.ds(k0, tk), :]
        kidx = col + k0
        mask = jnp.logical_and(kidx // CHUNK <= q_chunk, kidx < tk_valid)
        for h in range(2):
            k = k_refs[h][0, pl.ds(k0, tk), :]
            s = lax.dot_general(qs[h], k, (((1,), (1,)), ((), ())), preferred_element_type=F32) * scale
            s = jnp.where(mask, s, NEG_INF)
            m_prev = m_ref[h]
            m_next = jnp.maximum(m_prev, jnp.max(s, axis=1, keepdims=True))
            p = jnp.exp(s - _tile_lanes(m_next, tk // LANES))
            alpha = jnp.exp(m_prev - m_next)
            l_ref[h] = alpha * l_ref[h] + jnp.sum(p, axis=1, keepdims=True)
            acc_ref[h] = alpha * acc_ref[h] + jnp.dot(p.astype(BF16), v, preferred_element_type=F32)
            m_ref[h] = m_next
        return carry

    lax.fori_loop(0, nkb, body, 0)
    lane = lax.broadcasted_iota(jnp.int32, (tq, LANES), 1)
    o_ref[0] = jnp.where(lane < MLA_V_DIM, acc_ref[0] / l_ref[0], acc_ref[1] / l_ref[1]).astype(BF16)


def _mla_attention(q, k, v, q_off, tk_valid, tq, tk):
    b, t_q, _ = q.shape
    t_k = k.shape[1]
    nk = t_k // tk
    kern = functools.partial(_mla_kernel, tq=tq, tk=tk, nk=nk, q_off=q_off, tk_valid=tk_valid)
    q_spec = lambda o: pl.BlockSpec((1, tq, LANES), lambda bi, p, qi: (bi, qi, 2 * p + o))
    k_spec = lambda o: pl.BlockSpec((1, t_k, LANES), lambda bi, p, qi: (bi, 0, 2 * p + o))
    return pl.pallas_call(
        kern,
        grid=(b, MLA_HEADS // 2, t_q // tq),
        in_specs=[q_spec(0), q_spec(1), k_spec(0), k_spec(1),
                  pl.BlockSpec((1, t_k, LANES), lambda bi, p, qi: (bi, 0, p))],
        out_specs=pl.BlockSpec((1, tq, LANES), lambda bi, p, qi: (bi, qi, p)),
        out_shape=jax.ShapeDtypeStruct((b, t_q, MLA_WIDTH), BF16),
        scratch_shapes=[pltpu.VMEM((2, tq, LANES), F32)] * 3,
        compiler_params=_cparams(("parallel", "parallel", "arbitrary")),
        name="mla_attention",
    )(q, q, k, k, v)


def _merge_kernel(x_ref, oa_ref, ob_ref, wga_ref, wgb_ref, woa_ref, wob_ref, wout_ref, g_ref, b_ref, h_o, *, alpha):
    x = x_ref[...]
    xb = x.astype(BF16)
    ga = jax.nn.sigmoid(jnp.dot(xb, wga_ref[...], preferred_element_type=F32))
    gb = jax.nn.sigmoid(jnp.dot(xb, wgb_ref[...], preferred_element_type=F32))
    merged = (ga * jnp.dot(oa_ref[...], woa_ref[...], preferred_element_type=F32)
              + gb * jnp.dot(ob_ref[...], wob_ref[...], preferred_element_type=F32))
    y = alpha * x + jnp.dot(merged.astype(BF16), wout_ref[...], preferred_element_type=F32)
    h_o[...] = _layer_norm(y, g_ref[...], b_ref[...])


def _merge(x, oa, ob, w, tm, alpha):
    n = x.shape[0]
    tok = lambda width: pl.BlockSpec((tm, width), lambda i: (i, 0))
    weights = (w["w_ga"], w["w_gb"], w["w_oa"], w["w_ob"], w["w_out"], w["ln1_g"], w["ln1_b"])
    return pl.pallas_call(
        functools.partial(_merge_kernel, alpha=alpha),
        grid=(n // tm,),
        in_specs=[tok(D_MODEL), tok(SB_WIDTH), tok(MLA_WIDTH)] + [_full(a.shape) for a in weights],
        out_specs=tok(D_MODEL),
        out_shape=jax.ShapeDtypeStruct((n, D_MODEL), F32),
        compiler_params=_cparams(("parallel",)),
        name="merge",
    )(x, oa, ob, *weights)


PEER_TE = 1024
I1_PER_TILE = PEER_TE // PEER_N_KEYS


def _extract_top(s_ref, n_rows, tb):
    row = lax.broadcasted_iota(jnp.int32, (n_rows, tb), 0).astype(F32)
    slot = lax.broadcasted_iota(jnp.int32, (PEER_TOPK, tb), 0)

    def body(r, vals):
        s = s_ref[...]
        m = jnp.max(s, axis=0, keepdims=True)
        first = jnp.min(jnp.where(s == m, row, float(n_rows)), axis=0, keepdims=True)
        s_ref[...] = jnp.where(row == first, -jnp.inf, s)
        return jnp.where(slot == r, m, vals)

    return lax.fori_loop(0, PEER_TOPK, body, jnp.zeros((PEER_TOPK, tb), F32))


def _peer_kernel(h_ref, wqt_ref, keys_ref, u_ref, vt_ref, g_ref, b_ref, y_o,
                 htb_ref, qt_ref, s_ref, e_ref, tau_ref, top_ref, work_ref, cand_ref, a_ref, wt_ref, outt_ref,
                 *, tb, alpha):
    j = pl.program_id(1)

    @pl.when(j == 0)
    def _prologue():
        htb = h_ref[...].T.astype(BF16)
        htb_ref[...] = htb
        qt_ref[...] = jnp.dot(wqt_ref[...], htb, preferred_element_type=F32)
        outt_ref[...] = jnp.zeros_like(outt_ref)

        def per_half(hc, carry):
            r0 = pl.multiple_of(hc * PEER_HALF, PEER_HALF)
            q = qt_ref[pl.ds(r0, PEER_HALF), :].astype(BF16)
            s = jnp.dot(keys_ref[hc], q, preferred_element_type=F32)
            s_ref[hc] = s
            work_ref[...] = s
            top_ref[hc] = _extract_top(work_ref, PEER_N_KEYS, tb)
            return carry

        lax.fori_loop(0, 2 * PEER_HEADS, per_half, 0)

        def per_head(h, carry):
            t1 = top_ref[2 * h]
            t2 = top_ref[2 * h + 1]
            for a in range(PEER_TOPK):
                cand_ref[a * PEER_TOPK:(a + 1) * PEER_TOPK, :] = t1[a:a + 1, :] + t2
            top = _extract_top(cand_ref, PEER_TOPK * PEER_TOPK, tb)
            z = jnp.sum(jnp.exp(top - top[0:1, :]), axis=0, keepdims=True)
            tau_ref[h] = jnp.broadcast_to(top[PEER_TOPK - 1:PEER_TOPK, :], (8, tb))
            e_ref[2 * h] = jnp.exp(s_ref[2 * h] - t1[0:1, :])
            e_ref[2 * h + 1] = jnp.exp(s_ref[2 * h + 1] - t2[0:1, :]) / z
            return carry

        lax.fori_loop(0, PEER_HEADS, per_head, 0)

    a_ref[...] = jnp.dot(u_ref[...], htb_ref[...], preferred_element_type=F32)
    i1_0 = pl.multiple_of(j * I1_PER_TILE, I1_PER_TILE)
    sub = lax.broadcasted_iota(jnp.int32, (I1_PER_TILE, tb), 0)

    def per_i1(i, carry):
        r0 = pl.multiple_of(i * PEER_N_KEYS, PEER_N_KEYS)
        a = a_ref[pl.ds(r0, PEER_N_KEYS), :]
        act = 0.5 * a * (1.0 + lax.erf(a * (0.5 ** 0.5)))
        pick = sub == i
        gate = jnp.zeros((PEER_N_KEYS, tb), F32)
        for h in range(PEER_HEADS):
            s1 = jnp.sum(jnp.where(pick, s_ref[2 * h, pl.ds(i1_0, I1_PER_TILE), :], 0.0), axis=0, keepdims=True)
            e1 = jnp.sum(jnp.where(pick, e_ref[2 * h, pl.ds(i1_0, I1_PER_TILE), :], 0.0), axis=0, keepdims=True)
            score = s1 + s_ref[2 * h + 1]
            gate = gate + jnp.where(score >= tau_ref[h, 0:1, :], e1 * e_ref[2 * h + 1], 0.0)
        wt_ref[pl.ds(r0, PEER_N_KEYS), :] = (gate * act).astype(BF16)
        return carry

    lax.fori_loop(0, I1_PER_TILE, per_i1, 0)
    outt_ref[...] += jnp.dot(vt_ref[...], wt_ref[...], preferred_element_type=F32)

    @pl.when(j == pl.num_programs(1) - 1)
    def _epilogue():
        h = h_ref[...]
        y_o[...] = _layer_norm(alpha * h + outt_ref[...].T, g_ref[...], b_ref[...])


def _peer(h, w, tb, alpha):
    n = h.shape[0]
    n_exp = w["u"].shape[0]
    weights_pre = (w["wq_t"], w["keys"])
    return pl.pallas_call(
        functools.partial(_peer_kernel, tb=tb, alpha=alpha),
        grid=(n // tb, n_exp // PEER_TE),
        in_specs=[pl.BlockSpec((tb, D_MODEL), lambda i, j: (i, 0))]
        + [_full(a.shape) for a in weights_pre]
        + [pl.BlockSpec((PEER_TE, D_MODEL), lambda i, j: (j, 0)),
           pl.BlockSpec((D_MODEL, PEER_TE), lambda i, j: (0, j)),
           _full(w["ln2_g"].shape), _full(w["ln2_b"].shape)],
        out_specs=pl.BlockSpec((tb, D_MODEL), lambda i, j: (i, 0)),
        out_shape=jax.ShapeDtypeStruct((n, D_MODEL), F32),
        scratch_shapes=[
            pltpu.VMEM((D_MODEL, tb), BF16),
            pltpu.VMEM((2 * PEER_HEADS * PEER_HALF, tb), F32),
            pltpu.VMEM((2 * PEER_HEADS, PEER_N_KEYS, tb), F32),
            pltpu.VMEM((2 * PEER_HEADS, PEER_N_KEYS, tb), F32),
            pltpu.VMEM((PEER_HEADS, 8, tb), F32),
            pltpu.VMEM((2 * PEER_HEADS, PEER_TOPK, tb), F32),
            pltpu.VMEM((PEER_N_KEYS, tb), F32),
            pltpu.VMEM((PEER_TOPK * PEER_TOPK, tb), F32),
            pltpu.VMEM((PEER_TE, tb), F32),
            pltpu.VMEM((PEER_TE, tb), BF16),
            pltpu.VMEM((D_MODEL, tb), F32),
        ],
        compiler_params=_cparams(("parallel", "arbitrary")),
        name="peer",
    )(h, *weights_pre, w["u"], w["v_t"], w["ln2_g"], w["ln2_b"])


def _rope_tables(pos):
    inv = ROPE_THETA ** (-jnp.arange(ROPE_HALF, dtype=F32) / ROPE_HALF)
    ang = pos.astype(F32)[:, None] * inv[None, :]
    cos, sin = jnp.cos(ang), jnp.sin(ang)
    n = pos.shape[0]
    z16 = jnp.zeros((n, ROPE_HALF), F32)
    head = lambda fill: jnp.full((n, MLA_NOPE_DIM), fill, F32)
    tail = jnp.zeros((n, LANES - MLA_NOPE_DIM - MLA_ROPE_DIM), F32)
    cq = jnp.concatenate([head(1.0), cos, cos, tail], axis=1)
    ck = jnp.concatenate([head(0.0), cos, cos, tail], axis=1)
    s1 = jnp.concatenate([head(0.0), -sin, z16, tail], axis=1)
    s2 = jnp.concatenate([head(0.0), z16, sin, tail], axis=1)
    return cq, ck, s1, s2


def _pad_heads(wm, width):
    k = wm.shape[0]
    wm = wm.reshape(k, MLA_HEADS, width)
    return jnp.pad(wm, ((0, 0), (0, 0), (0, HEAD_PAD - width))).reshape(k, MLA_HEADS * HEAD_PAD)


def _prep_weights(w_in, q_norm_g, w_uq, kv_norm_g, w_uk, w_uv, w_oa, w_ob, w_out, ln1_g, ln1_b,
                  peer_wq, peer_sub_keys, peer_u, peer_v, ln2_g, ln2_b):
    o = 3 * SB_WIDTH + MLA_Q_LORA + MLA_KV_LORA
    w_kr = jnp.pad(w_in[:, o:o + MLA_ROPE_DIM], ((0, 0), (MLA_NOPE_DIM, LANES - MLA_NOPE_DIM - MLA_ROPE_DIM)))
    place = np.zeros((MLA_ROPE_DIM, LANES), np.float32)
    place[np.arange(MLA_ROPE_DIM), MLA_NOPE_DIM + np.arange(MLA_ROPE_DIM)] = 1.0
    og = o + MLA_ROPE_DIM
    row = lambda a: a.reshape(1, -1).astype(F32)
    return {
        "w_all": jnp.concatenate([w_in[:, :o], w_kr], axis=1).astype(BF16),
        "q_g": row(q_norm_g), "kv_g": row(kv_norm_g),
        "w_uq": _pad_heads(w_uq, MLA_NOPE_DIM + MLA_ROPE_DIM).astype(BF16),
        "w_uk": _pad_heads(w_uk, MLA_NOPE_DIM).astype(BF16),
        "w_uv": w_uv.astype(BF16),
        "place": jnp.asarray(place, dtype=BF16),
        "w_ga": w_in[:, og:og + D_MODEL].astype(BF16), "w_gb": w_in[:, og + D_MODEL:og + 2 * D_MODEL].astype(BF16),
        "w_oa": w_oa.astype(BF16), "w_ob": w_ob.astype(BF16), "w_out": w_out.astype(BF16),
        "ln1_g": row(ln1_g), "ln1_b": row(ln1_b), "ln2_g": row(ln2_g), "ln2_b": row(ln2_b),
        "wq_t": peer_wq.T.astype(BF16),
        "keys": peer_sub_keys.reshape(2 * PEER_HEADS, PEER_N_KEYS, PEER_HALF).astype(BF16),
        "u": peer_u.astype(BF16),
        "v_t": peer_v.T.astype(BF16),
    }


def _tile(n, pref):
    return pref if n % pref == 0 else n


def _layer(x, pos, past, w, alpha):
    b, t, _ = x.shape
    n = b * t
    tm = _tile(t, 512)
    xf = x.reshape(n, D_MODEL)
    sbq, sbk, sbv, sbkb, sbvb, ckv, kpe, qcat, kcat, vm = _proj(xf, _rope_tables(pos), w, tm)
    r3 = lambda a: a.reshape(b, t, a.shape[-1])
    q_sb, k_sb, v_sb, q_m, k_m, v_m = map(r3, (sbq, sbkb, sbvb, qcat, kcat, vm))
    if past is None:
        q_off, tk_valid = 0, t
    else:
        p_len = past[0].shape[1]
        flat = lambda a: a.reshape(b * p_len, -1)
        pk, pv, pkc, pvm = _cache_prep(flat(past[0]), flat(past[1]), flat(past[2]), flat(past[3]), w,
                                       _tile(b * p_len, 1024))
        p3 = lambda a: a.reshape(b, p_len, a.shape[-1])
        k_sb = jnp.concatenate([p3(pk), k_sb], axis=1)
        v_sb = jnp.concatenate([p3(pv), v_sb], axis=1)
        k_m = jnp.concatenate([p3(pkc), k_m], axis=1)
        v_m = jnp.concatenate([p3(pvm), v_m], axis=1)
        q_off, tk_valid = p_len, p_len + t
    tq = _tile(t, 256)
    tk = 256
    pad = (-k_sb.shape[1]) % tk
    if pad:
        padk = lambda a: jnp.pad(a, ((0, 0), (0, pad), (0, 0)))
        k_sb, v_sb, k_m, v_m = map(padk, (k_sb, v_sb, k_m, v_m))
    o_a = _sb_attention(q_sb, k_sb, v_sb, q_off, tq, tk)
    o_b = _mla_attention(q_m, k_m, v_m, q_off, tk_valid, tq, tk)
    h = _merge(xf, o_a.reshape(n, SB_WIDTH), o_b.reshape(n, MLA_WIDTH), w, tm, alpha)
    y = _peer(h, w, _tile(n, 512), alpha)
    state = (sbk.reshape(b, t, SB_HEADS, SB_HEAD_DIM), sbv.reshape(b, t, SB_HEADS, SB_HEAD_DIM),
             ckv.reshape(b, t, MLA_KV_LORA), kpe.reshape(b, t, MLA_ROPE_DIM))
    return y.reshape(b, t, D_MODEL), state


def kernel(x_prompt, x_sample, cache_sb_k, cache_sb_v, cache_mla_ckv, cache_mla_kpe, w_in, q_norm_g, w_uq,
           kv_norm_g, w_uk, w_uv, w_oa, w_ob, w_out, ln1_g, ln1_b, peer_wq, peer_sub_keys, peer_u, peer_v,
           ln2_g, ln2_b):
    depth = w_in.shape[0]
    alpha = (2.0 * depth) ** 0.25
    past_len = cache_sb_k.shape[2]
    pos_p = jnp.arange(x_prompt.shape[1], dtype=jnp.int32)
    pos_s = past_len + jnp.arange(x_sample.shape[1], dtype=jnp.int32)
    y_p, y_s = x_prompt, x_sample
    st_p, st_s = [], []
    for l in range(depth):
        w = _prep_weights(w_in[l], q_norm_g[l], w_uq[l], kv_norm_g[l], w_uk[l], w_uv[l], w_oa[l], w_ob[l],
                          w_out[l], ln1_g[l], ln1_b[l], peer_wq[l], peer_sub_keys[l], peer_u[l], peer_v[l],
                          ln2_g[l], ln2_b[l])
        y_p, sp = _layer(y_p, pos_p, None, w, alpha)
        sb_k_c = cache_sb_k[l].reshape(cache_sb_k.shape[1], past_len, SB_WIDTH)
        sb_v_c = cache_sb_v[l].reshape(cache_sb_v.shape[1], past_len, SB_WIDTH)
        y_s, ss = _layer(y_s, pos_s, (sb_k_c, sb_v_c, cache_mla_ckv[l], cache_mla_kpe[l]), w, alpha)
        st_p.append(sp)
        st_s.append(ss)
    stack = lambda sts, i: jnp.stack([s[i] for s in sts])
    return (y_p, y_s, stack(st_p, 0), stack(st_p, 1), stack(st_p, 2), stack(st_p, 3),
            stack(st_s, 0), stack(st_s, 1), stack(st_s, 2), stack(st_s, 3))
```

```python
import functools

import numpy as np
import jax
import jax.numpy as jnp
from jax import lax
from jax.experimental import pallas as pl
from jax.experimental.pallas import tpu as pltpu

F32 = jnp.float32
BF16 = jnp.bfloat16

D_MODEL = 1024
CHUNK = 64
SB_HEADS = 8
SB_HEAD_DIM = 64
SB_WIDTH = SB_HEADS * SB_HEAD_DIM
MLA_HEADS = 8
MLA_NOPE_DIM = 64
MLA_ROPE_DIM = 32
MLA_V_DIM = 64
MLA_Q_LORA = 384
MLA_KV_LORA = 256
MLA_WIDTH = MLA_HEADS * MLA_V_DIM
ROPE_THETA = 10000.0
PEER_HEADS = 8
PEER_N_KEYS = 128
PEER_TOPK = 16
PEER_HALF = 128
EPS = 1e-5
NEG_INF = -1e30
LOG2_E = 1.4426950408889634

LANES = 128
HEAD_PAD = LANES
ROPE_HALF = MLA_ROPE_DIM // 2
VMEM_LIMIT = 56 * 2**20


def _cparams(sem):
    return pltpu.CompilerParams(dimension_semantics=sem, vmem_limit_bytes=VMEM_LIMIT)


def _full(shape):
    nd = len(shape)
    return pl.BlockSpec(shape, lambda *_: (0,) * nd)


def _rms(x, g):
    ms = jnp.mean(x * x, axis=-1, keepdims=True)
    return x * lax.rsqrt(ms + EPS) * g


def _layer_norm(x, g, b):
    mu = jnp.mean(x, axis=-1, keepdims=True)
    xc = x - mu
    var = jnp.mean(xc * xc, axis=-1, keepdims=True)
    return xc * lax.rsqrt(var + EPS) * g + b


def _rope_lanes(x, c, s1, s2):
    return x * c + pltpu.roll(x, LANES - ROPE_HALF, 1) * s1 + pltpu.roll(x, ROPE_HALF, 1) * s2


W_ALL = 3 * SB_WIDTH + MLA_Q_LORA + MLA_KV_LORA + LANES


def _proj_kernel(x_ref, cq_ref, ck_ref, s1_ref, s2_ref, wall_ref, qg_ref, kvg_ref, wuq_ref, wuk_ref, wuv_ref,
                 sbq_o, sbk_o, sbv_o, sbkb_o, sbvb_o, ckv_o, kpe_o, qcat_o, kcat_o, vm_o):
    xb = x_ref[...].astype(BF16)
    proj = jnp.dot(xb, wall_ref[...], preferred_element_type=F32)
    sbq_o[...] = (proj[:, 0:SB_WIDTH] * (SB_HEAD_DIM ** -0.5)).astype(BF16)
    k = proj[:, SB_WIDTH:2 * SB_WIDTH]
    v = proj[:, 2 * SB_WIDTH:3 * SB_WIDTH]
    sbk_o[...] = k
    sbv_o[...] = v
    sbkb_o[...] = k.astype(BF16)
    sbvb_o[...] = v.astype(BF16)
    o = 3 * SB_WIDTH
    cqn = _rms(proj[:, o:o + MLA_Q_LORA], qg_ref[...])
    q = jnp.dot(cqn.astype(BF16), wuq_ref[...], preferred_element_type=F32)
    o += MLA_Q_LORA
    ckv = _rms(proj[:, o:o + MLA_KV_LORA], kvg_ref[...])
    ckv_o[...] = ckv
    o += MLA_KV_LORA
    cb = ckv.astype(BF16)
    kn = jnp.dot(cb, wuk_ref[...], preferred_element_type=F32)
    vm_o[...] = jnp.dot(cb, wuv_ref[...], preferred_element_type=F32).astype(BF16)
    s1 = s1_ref[...]
    s2 = s2_ref[...]
    kp = _rope_lanes(proj[:, o:o + LANES], ck_ref[...], s1, s2)
    kpe_o[...] = kp[:, MLA_NOPE_DIM:MLA_NOPE_DIM + MLA_ROPE_DIM]
    cq = cq_ref[...]
    for h in range(MLA_HEADS):
        sl = slice(h * HEAD_PAD, (h + 1) * HEAD_PAD)
        qcat_o[:, sl] = _rope_lanes(q[:, sl], cq, s1, s2).astype(BF16)
        kcat_o[:, sl] = (kn[:, sl] + kp).astype(BF16)


def _proj(x, tabs, w, tm):
    n = x.shape[0]
    rt = tabs[0].shape[0] // tm
    tok = lambda width: pl.BlockSpec((tm, width), lambda i: (i, 0))
    tab = pl.BlockSpec((tm, LANES), lambda i: (i % rt, 0))
    outs = [(SB_WIDTH, BF16), (SB_WIDTH, F32), (SB_WIDTH, F32), (SB_WIDTH, BF16), (SB_WIDTH, BF16),
            (MLA_KV_LORA, F32), (MLA_ROPE_DIM, F32), (MLA_HEADS * HEAD_PAD, BF16), (MLA_HEADS * HEAD_PAD, BF16),
            (MLA_WIDTH, BF16)]
    weights = (w["w_all"], w["q_g"], w["kv_g"], w["w_uq"], w["w_uk"], w["w_uv"])
    return pl.pallas_call(
        _proj_kernel,
        grid=(n // tm,),
        in_specs=[tok(D_MODEL), tab, tab, tab, tab] + [_full(a.shape) for a in weights],
        out_specs=[tok(wd) for wd, _ in outs],
        out_shape=[jax.ShapeDtypeStruct((n, wd), dt) for wd, dt in outs],
        compiler_params=_cparams(("parallel",)),
        name="proj",
    )(x, *tabs, *weights)


def _cache_kernel(ck_ref, cv_ref, ckv_ref, kpe_ref, wuk_ref, wuv_ref, place_ref, sbkb_o, sbvb_o, kcat_o, vm_o):
    sbkb_o[...] = ck_ref[...].astype(BF16)
    sbvb_o[...] = cv_ref[...].astype(BF16)
    cb = ckv_ref[...].astype(BF16)
    kn = jnp.dot(cb, wuk_ref[...], preferred_element_type=F32)
    vm_o[...] = jnp.dot(cb, wuv_ref[...], preferred_element_type=F32).astype(BF16)
    kp = jnp.dot(kpe_ref[...].astype(BF16), place_ref[...], preferred_element_type=F32)
    for h in range(MLA_HEADS):
        sl = slice(h * HEAD_PAD, (h + 1) * HEAD_PAD)
        kcat_o[:, sl] = (kn[:, sl] + kp).astype(BF16)


def _cache_prep(ck, cv, ckv, kpe, w, tm):
    n = ck.shape[0]
    tok = lambda width: pl.BlockSpec((tm, width), lambda i: (i, 0))
    outs = [(SB_WIDTH, BF16), (SB_WIDTH, BF16), (MLA_HEADS * HEAD_PAD, BF16), (MLA_WIDTH, BF16)]
    weights = (w["w_uk"], w["w_uv"], w["place"])
    return pl.pallas_call(
        _cache_kernel,
        grid=(n // tm,),
        in_specs=[tok(SB_WIDTH), tok(SB_WIDTH), tok(MLA_KV_LORA), tok(MLA_ROPE_DIM)]
        + [_full(a.shape) for a in weights],
        out_specs=[tok(wd) for wd, _ in outs],
        out_shape=[jax.ShapeDtypeStruct((n, wd), dt) for wd, dt in outs],
        compiler_params=_cparams(("parallel",)),
        name="cache_prep",
    )(ck, cv, ckv, kpe, *weights)


def _loop(n, step):
    def one(i, carry):
        step(i)
        return carry

    lax.fori_loop(0, n, one, 0)


def _loop_pairs(n, step):
    def pair(i, carry):
        step(2 * i)
        step(2 * i + 1)
        return carry

    lax.fori_loop(0, n // 2, pair, 0)
    _loop(n % 2, lambda _: step(n - 1))


def _sb_kernel(q_ref, k_ref, v_ref, tri_ref, o_ref, acc_ref, c_ref, *, tq, tk, nk, q_off):
    qi = pl.program_id(2)
    q = q_ref[0]
    lane = lax.broadcasted_iota(jnp.int32, (tq, LANES), 1)
    zero = jnp.zeros_like(q)
    qs = (jnp.where(lane < SB_HEAD_DIM, q, zero), jnp.where(lane >= SB_HEAD_DIM, q, zero))
    q_lo = q_off + qi * tq
    nkb = jnp.minimum(nk, (q_lo + tq - 1 + tk - 1) // tk)
    kb_diag = jnp.minimum(q_lo // tk, nkb)
    acc_ref[...] = jnp.zeros_like(acc_ref)
    c_ref[...] = jnp.zeros_like(c_ref)
    q_pos = lax.broadcasted_iota(jnp.int32, (tk, tq), 1) + q_lo
    krow = lax.broadcasted_iota(jnp.int32, (tk, tq), 0)
    tri = tri_ref[...]

    def block(kb, masked):
        k0 = pl.multiple_of(kb * tk, tk)
        k = k_ref[0, pl.ds(k0, tk), :]
        v = v_ref[0, pl.ds(k0, tk), :]
        if masked:
            causal = (krow + k0) < q_pos
        for h in range(2):
            z = lax.dot_general(k, qs[h], (((1,), (1,)), ((), ())), preferred_element_type=F32)
            lb = jnp.minimum(z, 0.0) - jnp.log(1.0 + jnp.exp(-jnp.abs(z)))
            lom = lb - z
            if masked:
                lom = jnp.where(causal, lom, 0.0)
            hi = lom.astype(BF16)
            lo = (lom - hi.astype(F32)).astype(BF16)
            suf = jnp.dot(tri, jnp.concatenate([hi, lo], axis=0), preferred_element_type=F32)
            c = c_ref[h, 0:1, :]
            a = jnp.exp(lb + suf + c)
            if masked:
                a = jnp.where(causal, a, 0.0)
            acc_ref[h] += lax.dot_general(v, a.astype(BF16), (((0,), (0,)), ((), ())), preferred_element_type=F32)
            c_ref[h] = jnp.broadcast_to(c + suf[0:1, :] + lom[0:1, :], (8, tq))

    _loop(nkb - kb_diag, lambda j: block(nkb - 1 - j, True))
    _loop_pairs(kb_diag, lambda j: block(kb_diag - 1 - j, False))
    sub = lax.broadcasted_iota(jnp.int32, (LANES, tq), 0)
    o_ref[0] = jnp.where(sub < SB_HEAD_DIM, acc_ref[0], acc_ref[1]).T.astype(BF16)


def _suffix_matrix(tk):
    s = np.arange(tk)[:, None]
    j = np.arange(tk)[None, :]
    tri = (j > s).astype(np.float32)
    return jnp.asarray(np.concatenate([tri, tri], axis=1), dtype=BF16)


def _sb_attention(q, k, v, q_off, tq, tk):
    b, t_q, _ = q.shape
    t_k = k.shape[1]
    nk = t_k // tk
    kern = functools.partial(_sb_kernel, tq=tq, tk=tk, nk=nk, q_off=q_off)
    kv_spec = pl.BlockSpec((1, t_k, LANES), lambda bi, p, qi: (bi, 0, p))
    return pl.pallas_call(
        kern,
        grid=(b, SB_WIDTH // LANES, t_q // tq),
        in_specs=[pl.BlockSpec((1, tq, LANES), lambda bi, p, qi: (bi, qi, p)), kv_spec, kv_spec,
                  _full((tk, 2 * tk))],
        out_specs=pl.BlockSpec((1, tq, LANES), lambda bi, p, qi: (bi, qi, p)),
        out_shape=jax.ShapeDtypeStruct((b, t_q, SB_WIDTH), BF16),
        scratch_shapes=[pltpu.VMEM((2, LANES, tq), F32), pltpu.VMEM((2, 8, tq), F32)],
        compiler_params=_cparams(("parallel", "parallel", "arbitrary")),
        name="sb_attention",
    )(q, k, v, _suffix_matrix(tk))


def _mla_kernel(qa_ref, qb_ref, ka_ref, kb_ref, v_ref, o_ref, acc_ref, m_ref, l_ref, *, tq, tk, nk, q_off, tk_valid):
    qi = pl.program_id(2)
    q_lo = q_off + qi * tq
    k_hi = ((q_lo + tq - 1) // CHUNK + 1) * CHUNK
    nkb = jnp.minimum(nk, (k_hi + tk - 1) // tk)
    n_free = jnp.minimum(jnp.minimum(((q_lo // CHUNK + 1) * CHUNK) // tk, tk_valid // tk), nkb)
    acc_ref[...] = jnp.zeros_like(acc_ref)
    l_ref[...] = jnp.zeros_like(l_ref)
    m_ref[...] = jnp.full_like(m_ref, NEG_INF)
    q_chunk = (lax.broadcasted_iota(jnp.int32, (tk, tq), 1) + q_lo) // CHUNK
    krow = lax.broadcasted_iota(jnp.int32, (tk, tq), 0)
    c2 = (MLA_NOPE_DIM + MLA_ROPE_DIM) ** -0.5 * LOG2_E
    qs = (qa_ref[0], qb_ref[0])
    k_refs = (ka_ref, kb_ref)

    def block(kb, masked):
        k0 = pl.multiple_of(kb * tk, tk)
        v = v_ref[0, pl.ds(k0, tk), :]
        if masked:
            kidx = krow + k0
            mask = jnp.where(kidx < tk_valid, kidx // CHUNK, q_chunk + 1) <= q_chunk
        for h in range(2):
            k = k_refs[h][0, pl.ds(k0, tk), :]
            raw = lax.dot_general(k, qs[h], (((1,), (1,)), ((), ())), preferred_element_type=F32)
            if masked:
                raw = jnp.where(mask, raw, NEG_INF)
            m_prev = m_ref[h, 0:1, :]
            m_next = jnp.maximum(m_prev, jnp.max(raw, axis=0, keepdims=True))
            p = jnp.exp2((raw - m_next) * c2)
            alpha = jnp.exp2((m_prev - m_next) * c2)
            l_next = alpha * l_ref[h, 0:1, :] + jnp.sum(p, axis=0, keepdims=True)
            pv = lax.dot_general(v, p.astype(BF16), (((0,), (0,)), ((), ())), preferred_element_type=F32)
            acc_ref[h] = alpha * acc_ref[h] + pv
            l_ref[h] = jnp.broadcast_to(l_next, (8, tq))
            m_ref[h] = jnp.broadcast_to(m_next, (8, tq))

    _loop_pairs(n_free, lambda kb: block(kb, False))
    _loop(nkb - n_free, lambda j: block(n_free + j, True))
    sub = lax.broadcasted_iota(jnp.int32, (LANES, tq), 0)
    out_t = jnp.where(sub < MLA_V_DIM, acc_ref[0] / l_ref[0, 0:1, :], acc_ref[1] / l_ref[1, 0:1, :])
    o_ref[0] = out_t.T.astype(BF16)


def _mla_attention(q, k, v, q_off, tk_valid, tq, tk):
    b, t_q, _ = q.shape
    t_k = k.shape[1]
    nk = t_k // tk
    kern = functools.partial(_mla_kernel, tq=tq, tk=tk, nk=nk, q_off=q_off, tk_valid=tk_valid)
    q_spec = lambda o: pl.BlockSpec((1, tq, LANES), lambda bi, p, qi: (bi, qi, 2 * p + o))
    k_spec = lambda o: pl.BlockSpec((1, t_k, LANES), lambda bi, p, qi: (bi, 0, 2 * p + o))
    return pl.pallas_call(
        kern,
        grid=(b, MLA_HEADS // 2, t_q // tq),
        in_specs=[q_spec(0), q_spec(1), k_spec(0), k_spec(1),
                  pl.BlockSpec((1, t_k, LANES), lambda bi, p, qi: (bi, 0, p))],
        out_specs=pl.BlockSpec((1, tq, LANES), lambda bi, p, qi: (bi, qi, p)),
        out_shape=jax.ShapeDtypeStruct((b, t_q, MLA_WIDTH), BF16),
        scratch_shapes=[pltpu.VMEM((2, LANES, tq), F32), pltpu.VMEM((2, 8, tq), F32), pltpu.VMEM((2, 8, tq), F32)],
        compiler_params=_cparams(("parallel", "parallel", "arbitrary")),
        name="mla_attention",
    )(q, q, k, k, v)


def _merge_kernel(x_ref, oa_ref, ob_ref, wga_ref, wgb_ref, woa_ref, wob_ref, wout_ref, g_ref, b_ref, h_o, *, alpha):
    x = x_ref[...]
    xb = x.astype(BF16)
    ga = jax.nn.sigmoid(jnp.dot(xb, wga_ref[...], preferred_element_type=F32))
    gb = jax.nn.sigmoid(jnp.dot(xb, wgb_ref[...], preferred_element_type=F32))
    merged = (ga * jnp.dot(oa_ref[...], woa_ref[...], preferred_element_type=F32)
              + gb * jnp.dot(ob_ref[...], wob_ref[...], preferred_element_type=F32))
    y = alpha * x + jnp.dot(merged.astype(BF16), wout_ref[...], preferred_element_type=F32)
    h_o[...] = _layer_norm(y, g_ref[...], b_ref[...])


def _merge(x, oa, ob, w, tm, alpha):
    n = x.shape[0]
    tok = lambda width: pl.BlockSpec((tm, width), lambda i: (i, 0))
    weights = (w["w_ga"], w["w_gb"], w["w_oa"], w["w_ob"], w["w_out"], w["ln1_g"], w["ln1_b"])
    return pl.pallas_call(
        functools.partial(_merge_kernel, alpha=alpha),
        grid=(n // tm,),
        in_specs=[tok(D_MODEL), tok(SB_WIDTH), tok(MLA_WIDTH)] + [_full(a.shape) for a in weights],
        out_specs=tok(D_MODEL),
        out_shape=jax.ShapeDtypeStruct((n, D_MODEL), F32),
        compiler_params=_cparams(("parallel",)),
        name="merge",
    )(x, oa, ob, *weights)


PEER_TE = 1024
I1_PER_TILE = PEER_TE // PEER_N_KEYS
W_CHUNK = 512
CAND_COUNTS = tuple(PEER_TOPK // (a + 1) for a in range(PEER_TOPK))
CAND_STARTS = tuple(int(x) for x in np.cumsum((0,) + CAND_COUNTS[:-1]))
N_CAND = sum(CAND_COUNTS)
CAND_ROWS = -(-N_CAND // 8) * 8


def _extract_rounds(s_refs, n_rows, tb):
    row = lax.broadcasted_iota(jnp.int32, (n_rows, tb), 0).astype(F32)
    slot = lax.broadcasted_iota(jnp.int32, (PEER_TOPK, tb), 0)

    def body(r, carry):
        out = []
        for s_ref, (vals, firsts) in zip(s_refs, carry):
            s = s_ref[...]
            m = jnp.max(s, axis=0, keepdims=True)
            first = jnp.min(jnp.where(s == m, row, float(n_rows)), axis=0, keepdims=True)
            s_ref[...] = jnp.where(row == first, -jnp.inf, s)
            out.append((jnp.where(slot == r, m, vals), jnp.where(slot == r, first, firsts)))
        return tuple(out)

    init = tuple((jnp.zeros((PEER_TOPK, tb), F32), jnp.zeros((PEER_TOPK, tb), F32)) for _ in s_refs)
    res = []
    for vals, firsts in lax.fori_loop(0, PEER_TOPK, body, init):
        rank = jnp.full((n_rows, tb), float(PEER_TOPK), F32)
        for r in range(PEER_TOPK):
            rank = jnp.where(row == firsts[r:r + 1, :], float(r), rank)
        res.append((vals, rank))
    return tuple(res)


def _peer_kernel(h_ref, wqt_ref, keys_ref, u_ref, vt_ref, g_ref, b_ref, y_o,
                 htb_ref, qt_ref, n1_ref, e1_ref, rank2_ref, e2_ref, work_ref, cand_ref, a_ref, outt_ref,
                 *, tb, alpha):
    j = pl.program_id(1)

    @pl.when(j == 0)
    def _prologue():
        htb = h_ref[...].T.astype(BF16)
        htb_ref[...] = htb
        qt_ref[...] = jnp.dot(wqt_ref[...], htb, preferred_element_type=F32)
        outt_ref[...] = jnp.zeros_like(outt_ref)
        a_ref[1] = jnp.zeros((PEER_TE, tb), F32)

        def per_head(h, carry):
            s = []
            for c in range(2):
                r0 = pl.multiple_of((2 * h + c) * PEER_HALF, PEER_HALF)
                q = qt_ref[pl.ds(r0, PEER_HALF), :].astype(BF16)
                s.append(jnp.dot(keys_ref[2 * h + c], q, preferred_element_type=F32))
                work_ref[c] = s[c]
            (t1, rank1), (t2, rank2) = _extract_rounds((work_ref.at[0], work_ref.at[1]), PEER_N_KEYS, tb)
            cand_ref[...] = jnp.full_like(cand_ref, -jnp.inf)
            for a in range(PEER_TOPK):
                cand_ref[CAND_STARTS[a]:CAND_STARTS[a] + CAND_COUNTS[a], :] = t1[a:a + 1, :] + t2[0:CAND_COUNTS[a], :]
            ((top, crank),) = _extract_rounds((cand_ref,), CAND_ROWS, tb)
            picked = jnp.where(crank < float(PEER_TOPK), 1.0, 0.0)
            n1 = jnp.zeros((PEER_N_KEYS, tb), F32)
            for a in range(PEER_TOPK):
                cnt = jnp.sum(picked[CAND_STARTS[a]:CAND_STARTS[a] + CAND_COUNTS[a], :], axis=0, keepdims=True)
                n1 = jnp.where(rank1 == float(a), cnt, n1)
            z = jnp.sum(jnp.exp(top - top[0:1, :]), axis=0, keepdims=True)
            n1_ref[h] = n1
            rank2_ref[h] = rank2.astype(BF16)
            e1_ref[h] = jnp.exp(s[0] - t1[0:1, :])
            e2_ref[h] = (jnp.exp(s[1] - t2[0:1, :]) / z).astype(BF16)
            return carry

        lax.fori_loop(0, PEER_HEADS, per_head, 0)

    slot = j % 2
    a_ref[slot] = jnp.dot(u_ref[...], htb_ref[...], preferred_element_type=F32)
    i1_0 = pl.multiple_of(jnp.maximum(j - 1, 0) * I1_PER_TILE, I1_PER_TILE)
    n1_rows = [n1_ref[h, pl.ds(i1_0, I1_PER_TILE), :] for h in range(PEER_HEADS)]
    e1_rows = [e1_ref[h, pl.ds(i1_0, I1_PER_TILE), :] for h in range(PEER_HEADS)]
    for c in range(PEER_TE // W_CHUNK):
        w = []
        for i in range(c * W_CHUNK // PEER_N_KEYS, (c + 1) * W_CHUNK // PEER_N_KEYS):
            a = a_ref[1 - slot, i * PEER_N_KEYS:(i + 1) * PEER_N_KEYS, :]
            act = 0.5 * a * (1.0 + lax.erf(a * (0.5 ** 0.5)))
            gate = jnp.zeros((PEER_N_KEYS, tb), BF16)
            for h in range(PEER_HEADS):
                n1b = jnp.broadcast_to(n1_rows[h][i:i + 1, :], (PEER_N_KEYS, tb)).astype(BF16)
                e1b = jnp.broadcast_to(e1_rows[h][i:i + 1, :], (PEER_N_KEYS, tb)).astype(BF16)
                gate = gate + jnp.where(rank2_ref[h] < n1b, e1b * e2_ref[h], jnp.zeros_like(gate))
            w.append(gate * act.astype(BF16))
        outt_ref[...] += jnp.dot(vt_ref[:, c * W_CHUNK:(c + 1) * W_CHUNK], jnp.concatenate(w, axis=0),
                                 preferred_element_type=F32)

    @pl.when(j == pl.num_programs(1) - 1)
    def _epilogue():
        h = h_ref[...]
        y_o[...] = _layer_norm(alpha * h + outt_ref[...].T, g_ref[...], b_ref[...])


def _peer(h, w, tb, alpha):
    n = h.shape[0]
    n_tiles = w["u"].shape[0] // PEER_TE
    weights_pre = (w["wq_t"], w["keys"])
    return pl.pallas_call(
        functools.partial(_peer_kernel, tb=tb, alpha=alpha),
        grid=(n // tb, n_tiles + 1),
        in_specs=[pl.BlockSpec((tb, D_MODEL), lambda i, j: (i, 0))]
        + [_full(a.shape) for a in weights_pre]
        + [pl.BlockSpec((PEER_TE, D_MODEL), lambda i, j: (jnp.minimum(j, n_tiles - 1), 0)),
           pl.BlockSpec((D_MODEL, PEER_TE), lambda i, j: (0, jnp.maximum(j - 1, 0))),
           _full(w["ln2_g"].shape), _full(w["ln2_b"].shape)],
        out_specs=pl.BlockSpec((tb, D_MODEL), lambda i, j: (i, 0)),
        out_shape=jax.ShapeDtypeStruct((n, D_MODEL), F32),
        scratch_shapes=[
            pltpu.VMEM((D_MODEL, tb), BF16),
            pltpu.VMEM((2 * PEER_HEADS * PEER_HALF, tb), F32),
            pltpu.VMEM((PEER_HEADS, PEER_N_KEYS, tb), F32),
            pltpu.VMEM((PEER_HEADS, PEER_N_KEYS, tb), F32),
            pltpu.VMEM((PEER_HEADS, PEER_N_KEYS, tb), BF16),
            pltpu.VMEM((PEER_HEADS, PEER_N_KEYS, tb), BF16),
            pltpu.VMEM((2, PEER_N_KEYS, tb), F32),
            pltpu.VMEM((CAND_ROWS, tb), F32),
            pltpu.VMEM((2, PEER_TE, tb), F32),
            pltpu.VMEM((D_MODEL, tb), F32),
        ],
        compiler_params=_cparams(("parallel", "arbitrary")),
        name="peer",
    )(h, *weights_pre, w["u"], w["v_t"], w["ln2_g"], w["ln2_b"])


def _rope_tables(pos):
    inv = ROPE_THETA ** (-jnp.arange(ROPE_HALF, dtype=F32) / ROPE_HALF)
    ang = pos.astype(F32)[:, None] * inv[None, :]
    cos, sin = jnp.cos(ang), jnp.sin(ang)
    n = pos.shape[0]
    z16 = jnp.zeros((n, ROPE_HALF), F32)
    head = lambda fill: jnp.full((n, MLA_NOPE_DIM), fill, F32)
    tail = jnp.zeros((n, LANES - MLA_NOPE_DIM - MLA_ROPE_DIM), F32)
    cq = jnp.concatenate([head(1.0), cos, cos, tail], axis=1)
    ck = jnp.concatenate([head(0.0), cos, cos, tail], axis=1)
    s1 = jnp.concatenate([head(0.0), -sin, z16, tail], axis=1)
    s2 = jnp.concatenate([head(0.0), z16, sin, tail], axis=1)
    return cq, ck, s1, s2


def _pad_heads(wm, width):
    k = wm.shape[0]
    wm = wm.reshape(k, MLA_HEADS, width)
    return jnp.pad(wm, ((0, 0), (0, 0), (0, HEAD_PAD - width))).reshape(k, MLA_HEADS * HEAD_PAD)


def _prep_weights(w_in, q_norm_g, w_uq, kv_norm_g, w_uk, w_uv, w_oa, w_ob, w_out, ln1_g, ln1_b,
                  peer_wq, peer_sub_keys, peer_u, peer_v, ln2_g, ln2_b):
    o = 3 * SB_WIDTH + MLA_Q_LORA + MLA_KV_LORA
    w_kr = jnp.pad(w_in[:, o:o + MLA_ROPE_DIM], ((0, 0), (MLA_NOPE_DIM, LANES - MLA_NOPE_DIM - MLA_ROPE_DIM)))
    place = np.zeros((MLA_ROPE_DIM, LANES), np.float32)
    place[np.arange(MLA_ROPE_DIM), MLA_NOPE_DIM + np.arange(MLA_ROPE_DIM)] = 1.0
    og = o + MLA_ROPE_DIM
    row = lambda a: a.reshape(1, -1).astype(F32)
    return {
        "w_all": jnp.concatenate([w_in[:, :o], w_kr], axis=1).astype(BF16),
        "q_g": row(q_norm_g), "kv_g": row(kv_norm_g),
        "w_uq": _pad_heads(w_uq, MLA_NOPE_DIM + MLA_ROPE_DIM).astype(BF16),
        "w_uk": _pad_heads(w_uk, MLA_NOPE_DIM).astype(BF16),
        "w_uv": w_uv.astype(BF16),
        "place": jnp.asarray(place, dtype=BF16),
        "w_ga": w_in[:, og:og + D_MODEL].astype(BF16), "w_gb": w_in[:, og + D_MODEL:og + 2 * D_MODEL].astype(BF16),
        "w_oa": w_oa.astype(BF16), "w_ob": w_ob.astype(BF16), "w_out": w_out.astype(BF16),
        "ln1_g": row(ln1_g), "ln1_b": row(ln1_b), "ln2_g": row(ln2_g), "ln2_b": row(ln2_b),
        "wq_t": peer_wq.T.astype(BF16),
        "keys": peer_sub_keys.reshape(2 * PEER_HEADS, PEER_N_KEYS, PEER_HALF).astype(BF16),
        "u": peer_u.astype(BF16),
        "v_t": peer_v.T.astype(BF16),
    }


def _tile(n, pref):
    return pref if n % pref == 0 else n


def _layer(x, pos, past, w, alpha):
    b, t, _ = x.shape
    n = b * t
    tm = _tile(t, 512)
    xf = x.reshape(n, D_MODEL)
    sbq, sbk, sbv, sbkb, sbvb, ckv, kpe, qcat, kcat, vm = _proj(xf, _rope_tables(pos), w, tm)
    r3 = lambda a: a.reshape(b, t, a.shape[-1])
    q_sb, k_sb, v_sb, q_m, k_m, v_m = map(r3, (sbq, sbkb, sbvb, qcat, kcat, vm))
    if past is None:
        q_off, tk_valid = 0, t
    else:
        p_len = past[0].shape[1]
        flat = lambda a: a.reshape(b * p_len, -1)
        pk, pv, pkc, pvm = _cache_prep(flat(past[0]), flat(past[1]), flat(past[2]), flat(past[3]), w,
                                       _tile(b * p_len, 1024))
        p3 = lambda a: a.reshape(b, p_len, a.shape[-1])
        k_sb = jnp.concatenate([p3(pk), k_sb], axis=1)
        v_sb = jnp.concatenate([p3(pv), v_sb], axis=1)
        k_m = jnp.concatenate([p3(pkc), k_m], axis=1)
        v_m = jnp.concatenate([p3(pvm), v_m], axis=1)
        q_off, tk_valid = p_len, p_len + t
    tq = _tile(t, 512)
    tk = 256
    pad = (-k_sb.shape[1]) % tk
    if pad:
        padk = lambda a: jnp.pad(a, ((0, 0), (0, pad), (0, 0)))
        k_sb, v_sb, k_m, v_m = map(padk, (k_sb, v_sb, k_m, v_m))
    o_a = _sb_attention(q_sb, k_sb, v_sb, q_off, tq, tk)
    o_b = _mla_attention(q_m, k_m, v_m, q_off, tk_valid, tq, tk)
    h = _merge(xf, o_a.reshape(n, SB_WIDTH), o_b.reshape(n, MLA_WIDTH), w, tm, alpha)
    y = _peer(h, w, _tile(n, 512), alpha)
    state = (sbk.reshape(b, t, SB_HEADS, SB_HEAD_DIM), sbv.reshape(b, t, SB_HEADS, SB_HEAD_DIM),
             ckv.reshape(b, t, MLA_KV_LORA), kpe.reshape(b, t, MLA_ROPE_DIM))
    return y.reshape(b, t, D_MODEL), state


def kernel(x_prompt, x_sample, cache_sb_k, cache_sb_v, cache_mla_ckv, cache_mla_kpe, w_in, q_norm_g, w_uq,
           kv_norm_g, w_uk, w_uv, w_oa, w_ob, w_out, ln1_g, ln1_b, peer_wq, peer_sub_keys, peer_u, peer_v,
           ln2_g, ln2_b):
    depth = w_in.shape[0]
    alpha = (2.0 * depth) ** 0.25
    past_len = cache_sb_k.shape[2]
    pos_p = jnp.arange(x_prompt.shape[1], dtype=jnp.int32)
    pos_s = past_len + jnp.arange(x_sample.shape[1], dtype=jnp.int32)
    y_p, y_s = x_prompt, x_sample
    st_p, st_s = [], []
    for l in range(depth):
        w = _prep_weights(w_in[l], q_norm_g[l], w_uq[l], kv_norm_g[l], w_uk[l], w_uv[l], w_oa[l], w_ob[l],
                          w_out[l], ln1_g[l], ln1_b[l], peer_wq[l], peer_sub_keys[l], peer_u[l], peer_v[l],
                          ln2_g[l], ln2_b[l])
        y_p, sp = _layer(y_p, pos_p, None, w, alpha)
        sb_k_c = cache_sb_k[l].reshape(cache_sb_k.shape[1], past_len, SB_WIDTH)
        sb_v_c = cache_sb_v[l].reshape(cache_sb_v.shape[1], past_len, SB_WIDTH)
        y_s, ss = _layer(y_s, pos_s, (sb_k_c, sb_v_c, cache_mla_ckv[l], cache_mla_kpe[l]), w, alpha)
        st_p.append(sp)
        st_s.append(ss)
    stack = lambda sts, i: jnp.stack([s[i] for s in sts])
    return (y_p, y_s, stack(st_p, 0), stack(st_p, 1), stack(st_p, 2), stack(st_p, 3),
            stack(st_s, 0), stack(st_s, 1), stack(st_s, 2), stack(st_s, 3))
```

```python
import functools

import numpy as np
import jax
import jax.numpy as jnp
from jax import lax
from jax.experimental import pallas as pl
from jax.experimental.pallas import tpu as pltpu

F32 = jnp.float32
BF16 = jnp.bfloat16

D_MODEL = 1024
CHUNK = 64
SB_HEADS = 8
SB_HEAD_DIM = 64
SB_WIDTH = SB_HEADS * SB_HEAD_DIM
MLA_HEADS = 8
MLA_NOPE_DIM = 64
MLA_ROPE_DIM = 32
MLA_V_DIM = 64
MLA_Q_LORA = 384
MLA_KV_LORA = 256
MLA_WIDTH = MLA_HEADS * MLA_V_DIM
ROPE_THETA = 10000.0
PEER_HEADS = 8
PEER_N_KEYS = 128
PEER_TOPK = 16
PEER_HALF = 128
EPS = 1e-5
NEG_INF = -1e30
LOG2_E = 1.4426950408889634
EXP_ZERO_F32 = -104.0

LANES = 128
HEAD_PAD = LANES
ROPE_HALF = MLA_ROPE_DIM // 2
VMEM_LIMIT = 56 * 2**20


def _cparams(sem):
    return pltpu.CompilerParams(dimension_semantics=sem, vmem_limit_bytes=VMEM_LIMIT)


def _full(shape):
    nd = len(shape)
    return pl.BlockSpec(shape, lambda *_: (0,) * nd)


def _rms(x, g):
    ms = jnp.mean(x * x, axis=-1, keepdims=True)
    return x * lax.rsqrt(ms + EPS) * g


def _layer_norm(x, g, b):
    mu = jnp.mean(x, axis=-1, keepdims=True)
    xc = x - mu
    var = jnp.mean(xc * xc, axis=-1, keepdims=True)
    return xc * lax.rsqrt(var + EPS) * g + b


def _rope_lanes(x, c, s1, s2):
    return x * c + pltpu.roll(x, LANES - ROPE_HALF, 1) * s1 + pltpu.roll(x, ROPE_HALF, 1) * s2


W_ALL = 3 * SB_WIDTH + MLA_Q_LORA + MLA_KV_LORA + LANES


def _proj_kernel(x_ref, cq_ref, ck_ref, s1_ref, s2_ref, wall_ref, qg_ref, kvg_ref, wuq_ref, wuk_ref, wuv_ref,
                 sbq_o, sbk_o, sbv_o, sbkb_o, sbvb_o, ckv_o, kpe_o, qcat_o, kcat_o, vm_o):
    xb = x_ref[...].astype(BF16)
    proj = jnp.dot(xb, wall_ref[...], preferred_element_type=F32)
    sbq_o[...] = (proj[:, 0:SB_WIDTH] * (SB_HEAD_DIM ** -0.5)).astype(BF16)
    k = proj[:, SB_WIDTH:2 * SB_WIDTH]
    v = proj[:, 2 * SB_WIDTH:3 * SB_WIDTH]
    sbk_o[...] = k
    sbv_o[...] = v
    sbkb_o[...] = k.astype(BF16)
    sbvb_o[...] = v.astype(BF16)
    o = 3 * SB_WIDTH
    cqn = _rms(proj[:, o:o + MLA_Q_LORA], qg_ref[...])
    q = jnp.dot(cqn.astype(BF16), wuq_ref[...], preferred_element_type=F32)
    o += MLA_Q_LORA
    ckv = _rms(proj[:, o:o + MLA_KV_LORA], kvg_ref[...])
    ckv_o[...] = ckv
    o += MLA_KV_LORA
    cb = ckv.astype(BF16)
    kn = jnp.dot(cb, wuk_ref[...], preferred_element_type=F32)
    vm_o[...] = jnp.dot(cb, wuv_ref[...], preferred_element_type=F32).astype(BF16)
    s1 = s1_ref[...]
    s2 = s2_ref[...]
    kp = _rope_lanes(proj[:, o:o + LANES], ck_ref[...], s1, s2)
    kpe_o[...] = kp[:, MLA_NOPE_DIM:MLA_NOPE_DIM + MLA_ROPE_DIM]
    cq = cq_ref[...]
    for h in range(MLA_HEADS):
        sl = slice(h * HEAD_PAD, (h + 1) * HEAD_PAD)
        qcat_o[:, sl] = _rope_lanes(q[:, sl], cq, s1, s2).astype(BF16)
        kcat_o[:, sl] = (kn[:, sl] + kp).astype(BF16)


def _proj(x, tabs, w, tm):
    n = x.shape[0]
    rt = tabs[0].shape[0] // tm
    tok = lambda width: pl.BlockSpec((tm, width), lambda i: (i, 0))
    tab = pl.BlockSpec((tm, LANES), lambda i: (i % rt, 0))
    outs = [(SB_WIDTH, BF16), (SB_WIDTH, F32), (SB_WIDTH, F32), (SB_WIDTH, BF16), (SB_WIDTH, BF16),
            (MLA_KV_LORA, F32), (MLA_ROPE_DIM, F32), (MLA_HEADS * HEAD_PAD, BF16), (MLA_HEADS * HEAD_PAD, BF16),
            (MLA_WIDTH, BF16)]
    weights = (w["w_all"], w["q_g"], w["kv_g"], w["w_uq"], w["w_uk"], w["w_uv"])
    return pl.pallas_call(
        _proj_kernel,
        grid=(n // tm,),
        in_specs=[tok(D_MODEL), tab, tab, tab, tab] + [_full(a.shape) for a in weights],
        out_specs=[tok(wd) for wd, _ in outs],
        out_shape=[jax.ShapeDtypeStruct((n, wd), dt) for wd, dt in outs],
        compiler_params=_cparams(("parallel",)),
        name="proj",
    )(x, *tabs, *weights)


def _cache_kernel(ck_ref, cv_ref, ckv_ref, kpe_ref, wuk_ref, wuv_ref, place_ref, sbkb_o, sbvb_o, kcat_o, vm_o):
    sbkb_o[...] = ck_ref[...].astype(BF16)
    sbvb_o[...] = cv_ref[...].astype(BF16)
    cb = ckv_ref[...].astype(BF16)
    kn = jnp.dot(cb, wuk_ref[...], preferred_element_type=F32)
    vm_o[...] = jnp.dot(cb, wuv_ref[...], preferred_element_type=F32).astype(BF16)
    kp = jnp.dot(kpe_ref[...].astype(BF16), place_ref[...], preferred_element_type=F32)
    for h in range(MLA_HEADS):
        sl = slice(h * HEAD_PAD, (h + 1) * HEAD_PAD)
        kcat_o[:, sl] = (kn[:, sl] + kp).astype(BF16)


def _cache_prep(ck, cv, ckv, kpe, w, tm):
    n = ck.shape[0]
    tok = lambda width: pl.BlockSpec((tm, width), lambda i: (i, 0))
    outs = [(SB_WIDTH, BF16), (SB_WIDTH, BF16), (MLA_HEADS * HEAD_PAD, BF16), (MLA_WIDTH, BF16)]
    weights = (w["w_uk"], w["w_uv"], w["place"])
    return pl.pallas_call(
        _cache_kernel,
        grid=(n // tm,),
        in_specs=[tok(SB_WIDTH), tok(SB_WIDTH), tok(MLA_KV_LORA), tok(MLA_ROPE_DIM)]
        + [_full(a.shape) for a in weights],
        out_specs=[tok(wd) for wd, _ in outs],
        out_shape=[jax.ShapeDtypeStruct((n, wd), dt) for wd, dt in outs],
        compiler_params=_cparams(("parallel",)),
        name="cache_prep",
    )(ck, cv, ckv, kpe, *weights)


def _loop(n, step):
    def one(i, carry):
        step(i)
        return carry

    lax.fori_loop(0, n, one, 0)


def _sb_kernel(q_ref, k_ref, v_ref, tri_ref, o_ref, acc_ref, c_ref, *, tq, tk, nk, q_off):
    qi = pl.program_id(2)
    q = q_ref[0]
    lane = lax.broadcasted_iota(jnp.int32, (tq, LANES), 1)
    zero = jnp.zeros_like(q)
    qs = (jnp.where(lane < SB_HEAD_DIM, q, zero), jnp.where(lane >= SB_HEAD_DIM, q, zero))
    q_lo = q_off + qi * tq
    nkb = jnp.minimum(nk, (q_lo + tq - 1 + tk - 1) // tk)
    kb_diag = jnp.minimum(q_lo // tk, nkb)
    acc_ref[...] = jnp.zeros_like(acc_ref)
    c_ref[...] = jnp.zeros_like(c_ref)
    q_pos = lax.broadcasted_iota(jnp.int32, (tk, tq), 1) + q_lo
    krow = lax.broadcasted_iota(jnp.int32, (tk, tq), 0)
    tri = tri_ref[...]

    def blocks(kbs, masked):
        k0s = [pl.multiple_of(kb * tk, tk) for kb in kbs]
        zs = [[lax.dot_general(k_ref[0, pl.ds(k0, tk), :], qs[h], (((1,), (1,)), ((), ())),
                               preferred_element_type=F32) for h in range(2)] for k0 in k0s]
        stage = []
        for k0, z2 in zip(k0s, zs):
            causal = (krow + k0) < q_pos if masked else None
            for z in z2:
                lb = jnp.minimum(z, 0.0) - jnp.log(1.0 + jnp.exp(-jnp.abs(z)))
                lom = lb - z
                if masked:
                    lom = jnp.where(causal, lom, 0.0)
                hi = lom.astype(BF16)
                lo = (lom - hi.astype(F32)).astype(BF16)
                suf = jnp.dot(tri, jnp.concatenate([hi, lo], axis=0), preferred_element_type=F32)
                stage.append((lb, suf, suf[0:1, :] + lom[0:1, :], causal))
        for b, k0 in enumerate(k0s):
            v = v_ref[0, pl.ds(k0, tk), :]
            for h in range(2):
                lb, suf, tot, causal = stage[2 * b + h]
                c = c_ref[h, 0:1, :]
                a = jnp.exp(lb + suf + c)
                if masked:
                    a = jnp.where(causal, a, 0.0)
                acc_ref[h] += lax.dot_general(v, a.astype(BF16), (((0,), (0,)), ((), ())),
                                              preferred_element_type=F32)
                c_ref[h] = jnp.broadcast_to(c + tot, (8, tq))

    n_diag = nkb - kb_diag
    _loop(n_diag // 2, lambda i: blocks((nkb - 1 - 2 * i, nkb - 2 - 2 * i), True))
    _loop(n_diag % 2, lambda _: blocks((kb_diag,), True))

    def live():
        return (jnp.max(c_ref[...]) > EXP_ZERO_F32).astype(jnp.int32)

    def more(state):
        return jnp.logical_and(state[0] < kb_diag, state[1] > 0)

    def back(state):
        blocks((kb_diag - 1 - state[0],), False)
        return state[0] + 1, live()

    lax.while_loop(more, back, (jnp.int32(0), live()))
    sub = lax.broadcasted_iota(jnp.int32, (LANES, tq), 0)
    o_ref[0] = jnp.where(sub < SB_HEAD_DIM, acc_ref[0], acc_ref[1]).T.astype(BF16)


def _suffix_matrix(tk):
    s = np.arange(tk)[:, None]
    j = np.arange(tk)[None, :]
    tri = (j > s).astype(np.float32)
    return jnp.asarray(np.concatenate([tri, tri], axis=1), dtype=BF16)


def _sb_attention(q, k, v, q_off, tq, tk):
    b, t_q, _ = q.shape
    t_k = k.shape[1]
    nk = t_k // tk
    kern = functools.partial(_sb_kernel, tq=tq, tk=tk, nk=nk, q_off=q_off)
    kv_spec = pl.BlockSpec((1, t_k, LANES), lambda bi, p, qi: (bi, 0, p))
    return pl.pallas_call(
        kern,
        grid=(b, SB_WIDTH // LANES, t_q // tq),
        in_specs=[pl.BlockSpec((1, tq, LANES), lambda bi, p, qi: (bi, qi, p)), kv_spec, kv_spec,
                  _full((tk, 2 * tk))],
        out_specs=pl.BlockSpec((1, tq, LANES), lambda bi, p, qi: (bi, qi, p)),
        out_shape=jax.ShapeDtypeStruct((b, t_q, SB_WIDTH), BF16),
        scratch_shapes=[pltpu.VMEM((2, LANES, tq), F32), pltpu.VMEM((2, 8, tq), F32)],
        compiler_params=_cparams(("parallel", "parallel", "arbitrary")),
        name="sb_attention",
    )(q, k, v, _suffix_matrix(tk))


def _mla_kernel(qa_ref, qb_ref, ka_ref, kb_ref, v_ref, o_ref, acc_ref, m_ref, l_ref, *, tq, tk, nk, q_off, tk_valid):
    qi = pl.program_id(2)
    q_lo = q_off + qi * tq
    k_hi = ((q_lo + tq - 1) // CHUNK + 1) * CHUNK
    nkb = jnp.minimum(nk, (k_hi + tk - 1) // tk)
    n_free = jnp.minimum(jnp.minimum(((q_lo // CHUNK + 1) * CHUNK) // tk, tk_valid // tk), nkb)
    acc_ref[...] = jnp.zeros_like(acc_ref)
    l_ref[...] = jnp.zeros_like(l_ref)
    m_ref[...] = jnp.full_like(m_ref, NEG_INF)
    q_chunk = (lax.broadcasted_iota(jnp.int32, (tk, tq), 1) + q_lo) // CHUNK
    krow = lax.broadcasted_iota(jnp.int32, (tk, tq), 0)
    c2 = (MLA_NOPE_DIM + MLA_ROPE_DIM) ** -0.5 * LOG2_E
    qs = (qa_ref[0], qb_ref[0])
    k_refs = (ka_ref, kb_ref)

    def blocks(kbs, masked):
        k0s = [pl.multiple_of(kb * tk, tk) for kb in kbs]
        raws = [[lax.dot_general(k_refs[h][0, pl.ds(k0, tk), :], qs[h], (((1,), (1,)), ((), ())),
                                 preferred_element_type=F32) for h in range(2)] for k0 in k0s]
        for k0, raw2 in zip(k0s, raws):
            v = v_ref[0, pl.ds(k0, tk), :]
            if masked:
                kidx = krow + k0
                mask = jnp.where(kidx < tk_valid, kidx // CHUNK, q_chunk + 1) <= q_chunk
            for h in range(2):
                raw = jnp.where(mask, raw2[h], NEG_INF) if masked else raw2[h]
                m_prev = m_ref[h, 0:1, :]
                m_next = jnp.maximum(m_prev, jnp.max(raw, axis=0, keepdims=True))
                p = jnp.exp2((raw - m_next) * c2)
                alpha = jnp.exp2((m_prev - m_next) * c2)
                l_next = alpha * l_ref[h, 0:1, :] + jnp.sum(p, axis=0, keepdims=True)
                pv = lax.dot_general(v, p.astype(BF16), (((0,), (0,)), ((), ())), preferred_element_type=F32)
                acc_ref[h] = alpha * acc_ref[h] + pv
                l_ref[h] = jnp.broadcast_to(l_next, (8, tq))
                m_ref[h] = jnp.broadcast_to(m_next, (8, tq))

    n4 = n_free // 4
    _loop(n4, lambda i: blocks(tuple(4 * i + d for d in range(4)), False))
    _loop((n_free % 4) // 2, lambda _: blocks((4 * n4, 4 * n4 + 1), False))
    _loop(n_free % 2, lambda _: blocks((n_free - 1,), False))
    n_edge = nkb - n_free
    _loop(n_edge // 2, lambda i: blocks((n_free + 2 * i, n_free + 2 * i + 1), True))
    _loop(n_edge % 2, lambda _: blocks((nkb - 1,), True))
    sub = lax.broadcasted_iota(jnp.int32, (LANES, tq), 0)
    out_t = jnp.where(sub < MLA_V_DIM, acc_ref[0] / l_ref[0, 0:1, :], acc_ref[1] / l_ref[1, 0:1, :])
    o_ref[0] = out_t.T.astype(BF16)


def _mla_attention(q, k, v, q_off, tk_valid, tq, tk):
    b, t_q, _ = q.shape
    t_k = k.shape[1]
    nk = t_k // tk
    kern = functools.partial(_mla_kernel, tq=tq, tk=tk, nk=nk, q_off=q_off, tk_valid=tk_valid)
    q_spec = lambda o: pl.BlockSpec((1, tq, LANES), lambda bi, p, qi: (bi, qi, 2 * p + o))
    k_spec = lambda o: pl.BlockSpec((1, t_k, LANES), lambda bi, p, qi: (bi, 0, 2 * p + o))
    return pl.pallas_call(
        kern,
        grid=(b, MLA_HEADS // 2, t_q // tq),
        in_specs=[q_spec(0), q_spec(1), k_spec(0), k_spec(1),
                  pl.BlockSpec((1, t_k, LANES), lambda bi, p, qi: (bi, 0, p))],
        out_specs=pl.BlockSpec((1, tq, LANES), lambda bi, p, qi: (bi, qi, p)),
        out_shape=jax.ShapeDtypeStruct((b, t_q, MLA_WIDTH), BF16),
        scratch_shapes=[pltpu.VMEM((2, LANES, tq), F32), pltpu.VMEM((2, 8, tq), F32), pltpu.VMEM((2, 8, tq), F32)],
        compiler_params=_cparams(("parallel", "parallel", "arbitrary")),
        name="mla_attention",
    )(q, q, k, k, v)


def _merge_kernel(x_ref, oa_ref, ob_ref, wga_ref, wgb_ref, woa_ref, wob_ref, wout_ref, g_ref, b_ref, h_o, *, alpha):
    x = x_ref[...]
    xb = x.astype(BF16)
    ga = jax.nn.sigmoid(jnp.dot(xb, wga_ref[...], preferred_element_type=F32))
    gb = jax.nn.sigmoid(jnp.dot(xb, wgb_ref[...], preferred_element_type=F32))
    merged = (ga * jnp.dot(oa_ref[...], woa_ref[...], preferred_element_type=F32)
              + gb * jnp.dot(ob_ref[...], wob_ref[...], preferred_element_type=F32))
    y = alpha * x + jnp.dot(merged.astype(BF16), wout_ref[...], preferred_element_type=F32)
    h_o[...] = _layer_norm(y, g_ref[...], b_ref[...])


def _merge(x, oa, ob, w, tm, alpha):
    n = x.shape[0]
    tok = lambda width: pl.BlockSpec((tm, width), lambda i: (i, 0))
    weights = (w["w_ga"], w["w_gb"], w["w_oa"], w["w_ob"], w["w_out"], w["ln1_g"], w["ln1_b"])
    return pl.pallas_call(
        functools.partial(_merge_kernel, alpha=alpha),
        grid=(n // tm,),
        in_specs=[tok(D_MODEL), tok(SB_WIDTH), tok(MLA_WIDTH)] + [_full(a.shape) for a in weights],
        out_specs=tok(D_MODEL),
        out_shape=jax.ShapeDtypeStruct((n, D_MODEL), F32),
        compiler_params=_cparams(("parallel",)),
        name="merge",
    )(x, oa, ob, *weights)


PEER_TE = 1024
I1_PER_TILE = PEER_TE // PEER_N_KEYS
W_CHUNK = 512
CAND_COUNTS = tuple(PEER_TOPK // (a + 1) for a in range(PEER_TOPK))
CAND_STARTS = tuple(int(x) for x in np.cumsum((0,) + CAND_COUNTS[:-1]))
N_CAND = sum(CAND_COUNTS)
CAND_ROWS = -(-N_CAND // 8) * 8


def _extract_rounds(s_refs, n_rows, tb):
    row = lax.broadcasted_iota(jnp.int32, (n_rows, tb), 0).astype(F32)
    slot = lax.broadcasted_iota(jnp.int32, (PEER_TOPK, tb), 0)

    def body(r, carry):
        out = []
        for s_ref, (vals, firsts) in zip(s_refs, carry):
            s = s_ref[...]
            m = jnp.max(s, axis=0, keepdims=True)
            first = jnp.min(jnp.where(s == m, row, float(n_rows)), axis=0, keepdims=True)
            s_ref[...] = jnp.where(row == first, -jnp.inf, s)
            out.append((jnp.where(slot == r, m, vals), jnp.where(slot == r, first, firsts)))
        return tuple(out)

    init = tuple((jnp.zeros((PEER_TOPK, tb), F32), jnp.zeros((PEER_TOPK, tb), F32)) for _ in s_refs)
    return lax.fori_loop(0, PEER_TOPK, body, init)


def _scatter_rows(firsts, values, n_rows, tb, fill):
    row = lax.broadcasted_iota(jnp.int32, (n_rows, tb), 0).astype(F32)
    out = jnp.full((n_rows, tb), fill, F32)
    for r in range(PEER_TOPK):
        out = jnp.where(row == firsts[r:r + 1, :], values[r], out)
    return out


def _peer_kernel(h_ref, wqt_ref, keys_ref, u_ref, vt_ref, g_ref, b_ref, y_o,
                 htb_ref, qt_ref, n1_ref, e1_ref, rank2_ref, e2_ref, work_ref, cand_ref, a0_ref, a1_ref, outt_ref,
                 *, tb, alpha):
    j = pl.program_id(1)

    @pl.when(j == 0)
    def _prologue():
        htb = h_ref[...].T.astype(BF16)
        htb_ref[...] = htb
        qt_ref[...] = jnp.dot(wqt_ref[...], htb, preferred_element_type=F32)
        outt_ref[...] = jnp.zeros_like(outt_ref)
        a1_ref[...] = jnp.zeros_like(a1_ref)

        def per_head(h, carry):
            s = []
            for c in range(2):
                r0 = pl.multiple_of((2 * h + c) * PEER_HALF, PEER_HALF)
                q = qt_ref[pl.ds(r0, PEER_HALF), :].astype(BF16)
                s.append(jnp.dot(keys_ref[2 * h + c], q, preferred_element_type=F32))
                work_ref[c] = s[c]
            (t1, first1), (t2, first2) = _extract_rounds((work_ref.at[0], work_ref.at[1]), PEER_N_KEYS, tb)
            cand_ref[...] = jnp.full_like(cand_ref, -jnp.inf)
            for a in range(PEER_TOPK):
                cand_ref[CAND_STARTS[a]:CAND_STARTS[a] + CAND_COUNTS[a], :] = t1[a:a + 1, :] + t2[0:CAND_COUNTS[a], :]
            ((top, cfirst),) = _extract_rounds((cand_ref,), CAND_ROWS, tb)
            counts = [jnp.sum(jnp.where(cfirst >= float(CAND_STARTS[a]),
                                        jnp.where(cfirst < float(CAND_STARTS[a] + CAND_COUNTS[a]), 1.0, 0.0), 0.0),
                              axis=0, keepdims=True) for a in range(PEER_TOPK)]
            z = jnp.sum(jnp.exp(top - top[0:1, :]), axis=0, keepdims=True)
            n1_ref[h] = _scatter_rows(first1, counts, PEER_N_KEYS, tb, 0.0)
            rank2 = _scatter_rows(first2, [float(r) for r in range(PEER_TOPK)], PEER_N_KEYS, tb, float(PEER_TOPK))
            rank2_ref[h] = rank2.astype(BF16)
            e1_ref[h] = jnp.exp(s[0] - t1[0:1, :])
            e2_ref[h] = (jnp.exp(s[1] - t2[0:1, :]) / z).astype(BF16)
            return carry

        lax.fori_loop(0, PEER_HEADS, per_head, 0)

    i1_0 = pl.multiple_of(jnp.maximum(j - 1, 0) * I1_PER_TILE, I1_PER_TILE)

    def step(a_next_ref, a_prev_ref):
        a_next_ref[...] = jnp.dot(u_ref[...], htb_ref[...], preferred_element_type=F32)
        n1_rows = [n1_ref[h, pl.ds(i1_0, I1_PER_TILE), :] for h in range(PEER_HEADS)]
        e1_rows = [e1_ref[h, pl.ds(i1_0, I1_PER_TILE), :] for h in range(PEER_HEADS)]
        for c in range(PEER_TE // W_CHUNK):
            w = []
            for i in range(c * W_CHUNK // PEER_N_KEYS, (c + 1) * W_CHUNK // PEER_N_KEYS):
                a = a_prev_ref[i * PEER_N_KEYS:(i + 1) * PEER_N_KEYS, :]
                act = 0.5 * a * (1.0 + lax.erf(a * (0.5 ** 0.5)))
                gate = jnp.zeros((PEER_N_KEYS, tb), BF16)
                for h in range(PEER_HEADS):
                    n1b = jnp.broadcast_to(n1_rows[h][i:i + 1, :], (PEER_N_KEYS, tb)).astype(BF16)
                    e1b = jnp.broadcast_to(e1_rows[h][i:i + 1, :], (PEER_N_KEYS, tb)).astype(BF16)
                    gate = gate + jnp.where(rank2_ref[h] < n1b, e1b * e2_ref[h], jnp.zeros_like(gate))
                w.append(gate * act.astype(BF16))
            outt_ref[...] += jnp.dot(vt_ref[:, c * W_CHUNK:(c + 1) * W_CHUNK], jnp.concatenate(w, axis=0),
                                     preferred_element_type=F32)

    @pl.when(j % 2 == 0)
    def _even():
        step(a0_ref, a1_ref)

    @pl.when(j % 2 == 1)
    def _odd():
        step(a1_ref, a0_ref)

    @pl.when(j == pl.num_programs(1) - 1)
    def _epilogue():
        h = h_ref[...]
        y_o[...] = _layer_norm(alpha * h + outt_ref[...].T, g_ref[...], b_ref[...])


def _peer(h, w, tb, alpha):
    n = h.shape[0]
    n_tiles = w["u"].shape[0] // PEER_TE
    weights_pre = (w["wq_t"], w["keys"])
    return pl.pallas_call(
        functools.partial(_peer_kernel, tb=tb, alpha=alpha),
        grid=(n // tb, n_tiles + 1),
        in_specs=[pl.BlockSpec((tb, D_MODEL), lambda i, j: (i, 0))]
        + [_full(a.shape) for a in weights_pre]
        + [pl.BlockSpec((PEER_TE, D_MODEL), lambda i, j: (jnp.minimum(j, n_tiles - 1), 0)),
           pl.BlockSpec((D_MODEL, PEER_TE), lambda i, j: (0, jnp.maximum(j - 1, 0))),
           _full(w["ln2_g"].shape), _full(w["ln2_b"].shape)],
        out_specs=pl.BlockSpec((tb, D_MODEL), lambda i, j: (i, 0)),
        out_shape=jax.ShapeDtypeStruct((n, D_MODEL), F32),
        scratch_shapes=[
            pltpu.VMEM((D_MODEL, tb), BF16),
            pltpu.VMEM((2 * PEER_HEADS * PEER_HALF, tb), F32),
            pltpu.VMEM((PEER_HEADS, PEER_N_KEYS, tb), F32),
            pltpu.VMEM((PEER_HEADS, PEER_N_KEYS, tb), F32),
            pltpu.VMEM((PEER_HEADS, PEER_N_KEYS, tb), BF16),
            pltpu.VMEM((PEER_HEADS, PEER_N_KEYS, tb), BF16),
            pltpu.VMEM((2, PEER_N_KEYS, tb), F32),
            pltpu.VMEM((CAND_ROWS, tb), F32),
            pltpu.VMEM((PEER_TE, tb), F32),
            pltpu.VMEM((PEER_TE, tb), F32),
            pltpu.VMEM((D_MODEL, tb), F32),
        ],
        compiler_params=_cparams(("parallel", "arbitrary")),
        name="peer",
    )(h, *weights_pre, w["u"], w["v_t"], w["ln2_g"], w["ln2_b"])


def _rope_tables(pos):
    inv = ROPE_THETA ** (-jnp.arange(ROPE_HALF, dtype=F32) / ROPE_HALF)
    ang = pos.astype(F32)[:, None] * inv[None, :]
    cos, sin = jnp.cos(ang), jnp.sin(ang)
    n = pos.shape[0]
    z16 = jnp.zeros((n, ROPE_HALF), F32)
    head = lambda fill: jnp.full((n, MLA_NOPE_DIM), fill, F32)
    tail = jnp.zeros((n, LANES - MLA_NOPE_DIM - MLA_ROPE_DIM), F32)
    cq = jnp.concatenate([head(1.0), cos, cos, tail], axis=1)
    ck = jnp.concatenate([head(0.0), cos, cos, tail], axis=1)
    s1 = jnp.concatenate([head(0.0), -sin, z16, tail], axis=1)
    s2 = jnp.concatenate([head(0.0), z16, sin, tail], axis=1)
    return cq, ck, s1, s2


def _pad_heads(wm, width):
    k = wm.shape[0]
    wm = wm.reshape(k, MLA_HEADS, width)
    return jnp.pad(wm, ((0, 0), (0, 0), (0, HEAD_PAD - width))).reshape(k, MLA_HEADS * HEAD_PAD)


def _prep_weights(w_in, q_norm_g, w_uq, kv_norm_g, w_uk, w_uv, w_oa, w_ob, w_out, ln1_g, ln1_b,
                  peer_wq, peer_sub_keys, peer_u, peer_v, ln2_g, ln2_b):
    o = 3 * SB_WIDTH + MLA_Q_LORA + MLA_KV_LORA
    w_kr = jnp.pad(w_in[:, o:o + MLA_ROPE_DIM], ((0, 0), (MLA_NOPE_DIM, LANES - MLA_NOPE_DIM - MLA_ROPE_DIM)))
    place = np.zeros((MLA_ROPE_DIM, LANES), np.float32)
    place[np.arange(MLA_ROPE_DIM), MLA_NOPE_DIM + np.arange(MLA_ROPE_DIM)] = 1.0
    og = o + MLA_ROPE_DIM
    row = lambda a: a.reshape(1, -1).astype(F32)
    return {
        "w_all": jnp.concatenate([w_in[:, :o], w_kr], axis=1).astype(BF16),
        "q_g": row(q_norm_g), "kv_g": row(kv_norm_g),
        "w_uq": _pad_heads(w_uq, MLA_NOPE_DIM + MLA_ROPE_DIM).astype(BF16),
        "w_uk": _pad_heads(w_uk, MLA_NOPE_DIM).astype(BF16),
        "w_uv": w_uv.astype(BF16),
        "place": jnp.asarray(place, dtype=BF16),
        "w_ga": w_in[:, og:og + D_MODEL].astype(BF16), "w_gb": w_in[:, og + D_MODEL:og + 2 * D_MODEL].astype(BF16),
        "w_oa": w_oa.astype(BF16), "w_ob": w_ob.astype(BF16), "w_out": w_out.astype(BF16),
        "ln1_g": row(ln1_g), "ln1_b": row(ln1_b), "ln2_g": row(ln2_g), "ln2_b": row(ln2_b),
        "wq_t": peer_wq.T.astype(BF16),
        "keys": peer_sub_keys.reshape(2 * PEER_HEADS, PEER_N_KEYS, PEER_HALF).astype(BF16),
        "u": peer_u.astype(BF16),
        "v_t": peer_v.T.astype(BF16),
    }


def _tile(n, pref):
    return pref if n % pref == 0 else n


def _layer(x, pos, past, w, alpha):
    b, t, _ = x.shape
    n = b * t
    tm = _tile(t, 512)
    xf = x.reshape(n, D_MODEL)
    sbq, sbk, sbv, sbkb, sbvb, ckv, kpe, qcat, kcat, vm = _proj(xf, _rope_tables(pos), w, tm)
    r3 = lambda a: a.reshape(b, t, a.shape[-1])
    q_sb, k_sb, v_sb, q_m, k_m, v_m = map(r3, (sbq, sbkb, sbvb, qcat, kcat, vm))
    if past is None:
        q_off, tk_valid = 0, t
    else:
        p_len = past[0].shape[1]
        flat = lambda a: a.reshape(b * p_len, -1)
        pk, pv, pkc, pvm = _cache_prep(flat(past[0]), flat(past[1]), flat(past[2]), flat(past[3]), w,
                                       _tile(b * p_len, 1024))
        p3 = lambda a: a.reshape(b, p_len, a.shape[-1])
        k_sb = jnp.concatenate([p3(pk), k_sb], axis=1)
        v_sb = jnp.concatenate([p3(pv), v_sb], axis=1)
        k_m = jnp.concatenate([p3(pkc), k_m], axis=1)
        v_m = jnp.concatenate([p3(pvm), v_m], axis=1)
        q_off, tk_valid = p_len, p_len + t
    tq = _tile(t, 512)
    tk = 256
    pad = (-k_sb.shape[1]) % tk
    if pad:
        padk = lambda a: jnp.pad(a, ((0, 0), (0, pad), (0, 0)))
        k_sb, v_sb, k_m, v_m = map(padk, (k_sb, v_sb, k_m, v_m))
    o_a = _sb_attention(q_sb, k_sb, v_sb, q_off, tq, tk)
    o_b = _mla_attention(q_m, k_m, v_m, q_off, tk_valid, tq, tk)
    h = _merge(xf, o_a.reshape(n, SB_WIDTH), o_b.reshape(n, MLA_WIDTH), w, tm, alpha)
    y = _peer(h, w, _tile(n, 512), alpha)
    state = (sbk.reshape(b, t, SB_HEADS, SB_HEAD_DIM), sbv.reshape(b, t, SB_HEADS, SB_HEAD_DIM),
             ckv.reshape(b, t, MLA_KV_LORA), kpe.reshape(b, t, MLA_ROPE_DIM))
    return y.reshape(b, t, D_MODEL), state


def kernel(x_prompt, x_sample, cache_sb_k, cache_sb_v, cache_mla_ckv, cache_mla_kpe, w_in, q_norm_g, w_uq,
           kv_norm_g, w_uk, w_uv, w_oa, w_ob, w_out, ln1_g, ln1_b, peer_wq, peer_sub_keys, peer_u, peer_v,
           ln2_g, ln2_b):
    depth = w_in.shape[0]
    alpha = (2.0 * depth) ** 0.25
    past_len = cache_sb_k.shape[2]
    pos_p = jnp.arange(x_prompt.shape[1], dtype=jnp.int32)
    pos_s = past_len + jnp.arange(x_sample.shape[1], dtype=jnp.int32)
    y_p, y_s = x_prompt, x_sample
    st_p, st_s = [], []
    for l in range(depth):
        w = _prep_weights(w_in[l], q_norm_g[l], w_uq[l], kv_norm_g[l], w_uk[l], w_uv[l], w_oa[l], w_ob[l],
                          w_out[l], ln1_g[l], ln1_b[l], peer_wq[l], peer_sub_keys[l], peer_u[l], peer_v[l],
                          ln2_g[l], ln2_b[l])
        y_p, sp = _layer(y_p, pos_p, None, w, alpha)
        sb_k_c = cache_sb_k[l].reshape(cache_sb_k.shape[1], past_len, SB_WIDTH)
        sb_v_c = cache_sb_v[l].reshape(cache_sb_v.shape[1], past_len, SB_WIDTH)
        y_s, ss = _layer(y_s, pos_s, (sb_k_c, sb_v_c, cache_mla_ckv[l], cache_mla_kpe[l]), w, alpha)
        st_p.append(sp)
        st_s.append(ss)
    stack = lambda sts, i: jnp.stack([s[i] for s in sts])
    return (y_p, y_s, stack(st_p, 0), stack(st_p, 1), stack(st_p, 2), stack(st_p, 3),
            stack(st_s, 0), stack(st_s, 1), stack(st_s, 2), stack(st_s, 3))
```

```python
import functools

import numpy as np
import jax
import jax.numpy as jnp
from jax import lax
from jax.experimental import pallas as pl
from jax.experimental.pallas import tpu as pltpu

F32 = jnp.float32
BF16 = jnp.bfloat16

D_MODEL = 1024
CHUNK = 64
SB_HEADS = 8
SB_HEAD_DIM = 64
SB_WIDTH = SB_HEADS * SB_HEAD_DIM
MLA_HEADS = 8
MLA_NOPE_DIM = 64
MLA_ROPE_DIM = 32
MLA_V_DIM = 64
MLA_Q_LORA = 384
MLA_KV_LORA = 256
MLA_WIDTH = MLA_HEADS * MLA_V_DIM
ROPE_THETA = 10000.0
PEER_HEADS = 8
PEER_N_KEYS = 128
PEER_TOPK = 16
PEER_HALF = 128
EPS = 1e-5
NEG_INF = -1e30
LOG2_E = 1.4426950408889634
EXP_ZERO_F32 = -104.0

LANES = 128
HEAD_PAD = LANES
ROPE_HALF = MLA_ROPE_DIM // 2
VMEM_LIMIT = 56 * 2**20


def _cparams(sem):
    return pltpu.CompilerParams(dimension_semantics=sem, vmem_limit_bytes=VMEM_LIMIT)


def _full(shape):
    nd = len(shape)
    return pl.BlockSpec(shape, lambda *_: (0,) * nd)


def _rms(x, g):
    ms = jnp.mean(x * x, axis=-1, keepdims=True)
    return x * lax.rsqrt(ms + EPS) * g


def _layer_norm(x, g, b):
    mu = jnp.mean(x, axis=-1, keepdims=True)
    xc = x - mu
    var = jnp.mean(xc * xc, axis=-1, keepdims=True)
    return xc * lax.rsqrt(var + EPS) * g + b


def _rope_lanes(x, c, s1, s2):
    return x * c + pltpu.roll(x, LANES - ROPE_HALF, 1) * s1 + pltpu.roll(x, ROPE_HALF, 1) * s2


W_ALL = 3 * SB_WIDTH + MLA_Q_LORA + MLA_KV_LORA + LANES


def _proj_kernel(x_ref, cq_ref, ck_ref, s1_ref, s2_ref, wall_ref, qg_ref, kvg_ref, wuq_ref, wuk_ref, wuv_ref,
                 sbq_o, sbk_o, sbv_o, sbkb_o, sbvb_o, ckv_o, kpe_o, qcat_o, kcat_o, vm_o):
    xb = x_ref[...].astype(BF16)
    proj = jnp.dot(xb, wall_ref[...], preferred_element_type=F32)
    sbq_o[...] = (proj[:, 0:SB_WIDTH] * (SB_HEAD_DIM ** -0.5)).astype(BF16)
    k = proj[:, SB_WIDTH:2 * SB_WIDTH]
    v = proj[:, 2 * SB_WIDTH:3 * SB_WIDTH]
    for h in range(SB_HEADS):
        sl = slice(h * SB_HEAD_DIM, (h + 1) * SB_HEAD_DIM)
        sbk_o[:, h, :] = k[:, sl]
        sbv_o[:, h, :] = v[:, sl]
    sbkb_o[...] = k.astype(BF16)
    sbvb_o[...] = v.astype(BF16)
    o = 3 * SB_WIDTH
    cqn = _rms(proj[:, o:o + MLA_Q_LORA], qg_ref[...])
    q = jnp.dot(cqn.astype(BF16), wuq_ref[...], preferred_element_type=F32)
    o += MLA_Q_LORA
    ckv = _rms(proj[:, o:o + MLA_KV_LORA], kvg_ref[...])
    ckv_o[...] = ckv
    o += MLA_KV_LORA
    cb = ckv.astype(BF16)
    kn = jnp.dot(cb, wuk_ref[...], preferred_element_type=F32)
    vm_o[...] = jnp.dot(cb, wuv_ref[...], preferred_element_type=F32).astype(BF16)
    s1 = s1_ref[...]
    s2 = s2_ref[...]
    kp = _rope_lanes(proj[:, o:o + LANES], ck_ref[...], s1, s2)
    kpe_o[...] = kp[:, MLA_NOPE_DIM:MLA_NOPE_DIM + MLA_ROPE_DIM]
    cq = cq_ref[...]
    for h in range(MLA_HEADS):
        sl = slice(h * HEAD_PAD, (h + 1) * HEAD_PAD)
        qcat_o[:, sl] = _rope_lanes(q[:, sl], cq, s1, s2).astype(BF16)
        kcat_o[:, sl] = (kn[:, sl] + kp).astype(BF16)


def _proj(x, tabs, w, tm):
    n = x.shape[0]
    rt = tabs[0].shape[0] // tm
    tok = lambda width: pl.BlockSpec((tm, width), lambda i: (i, 0))
    tab = pl.BlockSpec((tm, LANES), lambda i: (i % rt, 0))
    outs = [(SB_WIDTH, BF16), (SB_WIDTH, F32), (SB_WIDTH, F32), (SB_WIDTH, BF16), (SB_WIDTH, BF16),
            (MLA_KV_LORA, F32), (MLA_ROPE_DIM, F32), (MLA_HEADS * HEAD_PAD, BF16), (MLA_HEADS * HEAD_PAD, BF16),
            (MLA_WIDTH, BF16)]
    weights = (w["w_all"], w["q_g"], w["kv_g"], w["w_uq"], w["w_uk"], w["w_uv"])
    out_specs = [tok(wd) for wd, _ in outs]
    out_shape = [jax.ShapeDtypeStruct((n, wd), dt) for wd, dt in outs]
    for i in (1, 2):
        out_specs[i] = pl.BlockSpec((tm, SB_HEADS, SB_HEAD_DIM), lambda i: (i, 0, 0))
        out_shape[i] = jax.ShapeDtypeStruct((n, SB_HEADS, SB_HEAD_DIM), F32)
    return pl.pallas_call(
        _proj_kernel,
        grid=(n // tm,),
        in_specs=[tok(D_MODEL), tab, tab, tab, tab] + [_full(a.shape) for a in weights],
        out_specs=out_specs,
        out_shape=out_shape,
        compiler_params=_cparams(("parallel",)),
        name="proj",
    )(x, *tabs, *weights)


def _cache_kernel(ck_ref, cv_ref, ckv_ref, kpe_ref, wuk_ref, wuv_ref, place_ref, sbkb_o, sbvb_o, kcat_o, vm_o):
    sbkb_o[...] = jnp.concatenate([ck_ref[:, h, :] for h in range(SB_HEADS)], axis=1).astype(BF16)
    sbvb_o[...] = jnp.concatenate([cv_ref[:, h, :] for h in range(SB_HEADS)], axis=1).astype(BF16)
    cb = ckv_ref[...].astype(BF16)
    kn = jnp.dot(cb, wuk_ref[...], preferred_element_type=F32)
    vm_o[...] = jnp.dot(cb, wuv_ref[...], preferred_element_type=F32).astype(BF16)
    kp = jnp.dot(kpe_ref[...].astype(BF16), place_ref[...], preferred_element_type=F32)
    for h in range(MLA_HEADS):
        sl = slice(h * HEAD_PAD, (h + 1) * HEAD_PAD)
        kcat_o[:, sl] = (kn[:, sl] + kp).astype(BF16)


def _cache_prep(ck, cv, ckv, kpe, w, tm):
    n = ck.shape[0]
    tok = lambda width: pl.BlockSpec((tm, width), lambda i: (i, 0))
    outs = [(SB_WIDTH, BF16), (SB_WIDTH, BF16), (MLA_HEADS * HEAD_PAD, BF16), (MLA_WIDTH, BF16)]
    weights = (w["w_uk"], w["w_uv"], w["place"])
    return pl.pallas_call(
        _cache_kernel,
        grid=(n // tm,),
        in_specs=[pl.BlockSpec((tm, SB_HEADS, SB_HEAD_DIM), lambda i: (i, 0, 0))] * 2
        + [tok(MLA_KV_LORA), tok(MLA_ROPE_DIM)] + [_full(a.shape) for a in weights],
        out_specs=[tok(wd) for wd, _ in outs],
        out_shape=[jax.ShapeDtypeStruct((n, wd), dt) for wd, dt in outs],
        compiler_params=_cparams(("parallel",)),
        name="cache_prep",
    )(ck, cv, ckv, kpe, *weights)


def _loop(n, step):
    def one(i, carry):
        step(i)
        return carry

    lax.fori_loop(0, n, one, 0)


def _sb_kernel(q_ref, k_ref, v_ref, tri_ref, o_ref, acc_ref, c_ref, *, tq, tk, nk, q_off):
    qi = pl.program_id(2)
    q = q_ref[0]
    lane = lax.broadcasted_iota(jnp.int32, (tq, LANES), 1)
    zero = jnp.zeros_like(q)
    qs = (jnp.where(lane < SB_HEAD_DIM, q, zero), jnp.where(lane >= SB_HEAD_DIM, q, zero))
    q_lo = q_off + qi * tq
    nkb = jnp.minimum(nk, (q_lo + tq - 1 + tk - 1) // tk)
    kb_diag = jnp.minimum(q_lo // tk, nkb)
    acc_ref[...] = jnp.zeros_like(acc_ref)
    c_ref[...] = jnp.zeros_like(c_ref)
    q_pos = lax.broadcasted_iota(jnp.int32, (tk, tq), 1) + q_lo
    krow = lax.broadcasted_iota(jnp.int32, (tk, tq), 0)
    tri = tri_ref[...]

    def blocks(kbs, masked):
        k0s = [pl.multiple_of(kb * tk, tk) for kb in kbs]
        zs = [[lax.dot_general(k_ref[0, pl.ds(k0, tk), :], qs[h], (((1,), (1,)), ((), ())),
                               preferred_element_type=F32) for h in range(2)] for k0 in k0s]
        stage = []
        for k0, z2 in zip(k0s, zs):
            causal = (krow + k0) < q_pos if masked else None
            for z in z2:
                lb = jnp.minimum(z, 0.0) - jnp.log(1.0 + jnp.exp(-jnp.abs(z)))
                lom = lb - z
                if masked:
                    lom = jnp.where(causal, lom, 0.0)
                hi = lom.astype(BF16)
                lo = (lom - hi.astype(F32)).astype(BF16)
                suf = jnp.dot(tri, jnp.concatenate([hi, lo], axis=0), preferred_element_type=F32)
                stage.append((lb, suf, suf[0:1, :] + lom[0:1, :], causal))
        for b, k0 in enumerate(k0s):
            v = v_ref[0, pl.ds(k0, tk), :]
            for h in range(2):
                lb, suf, tot, causal = stage[2 * b + h]
                c = c_ref[h, 0:1, :]
                a = jnp.exp(lb + suf + c)
                if masked:
                    a = jnp.where(causal, a, 0.0)
                acc_ref[h] += lax.dot_general(v, a.astype(BF16), (((0,), (0,)), ((), ())),
                                              preferred_element_type=F32)
                c_ref[h] = jnp.broadcast_to(c + tot, (8, tq))

    n_diag = nkb - kb_diag
    _loop(n_diag // 2, lambda i: blocks((nkb - 1 - 2 * i, nkb - 2 - 2 * i), True))
    _loop(n_diag % 2, lambda _: blocks((kb_diag,), True))

    def live():
        return (jnp.max(c_ref[...]) > EXP_ZERO_F32).astype(jnp.int32)

    def more(state):
        return jnp.logical_and(state[0] < kb_diag, state[1] > 0)

    def back(state):
        blocks((kb_diag - 1 - state[0],), False)
        return state[0] + 1, live()

    lax.while_loop(more, back, (jnp.int32(0), live()))
    sub = lax.broadcasted_iota(jnp.int32, (LANES, tq), 0)
    o_ref[0] = jnp.where(sub < SB_HEAD_DIM, acc_ref[0], acc_ref[1]).T.astype(BF16)


def _suffix_matrix(tk):
    s = np.arange(tk)[:, None]
    j = np.arange(tk)[None, :]
    tri = (j > s).astype(np.float32)
    return jnp.asarray(np.concatenate([tri, tri], axis=1), dtype=BF16)


def _sb_attention(q, k, v, q_off, tq, tk):
    b, t_q, _ = q.shape
    t_k = k.shape[1]
    nk = t_k // tk
    kern = functools.partial(_sb_kernel, tq=tq, tk=tk, nk=nk, q_off=q_off)
    kv_spec = pl.BlockSpec((1, t_k, LANES), lambda bi, p, qi: (bi, 0, p))
    return pl.pallas_call(
        kern,
        grid=(b, SB_WIDTH // LANES, t_q // tq),
        in_specs=[pl.BlockSpec((1, tq, LANES), lambda bi, p, qi: (bi, qi, p)), kv_spec, kv_spec,
                  _full((tk, 2 * tk))],
        out_specs=pl.BlockSpec((1, tq, LANES), lambda bi, p, qi: (bi, qi, p)),
        out_shape=jax.ShapeDtypeStruct((b, t_q, SB_WIDTH), BF16),
        scratch_shapes=[pltpu.VMEM((2, LANES, tq), F32), pltpu.VMEM((2, 8, tq), F32)],
        compiler_params=_cparams(("parallel", "parallel", "arbitrary")),
        name="sb_attention",
    )(q, k, v, _suffix_matrix(tk))


def _mla_kernel(qa_ref, qb_ref, ka_ref, kb_ref, v_ref, o_ref, acc_ref, m_ref, l_ref, *, tq, tk, nk, q_off, tk_valid):
    qi = pl.program_id(2)
    q_lo = q_off + qi * tq
    k_hi = ((q_lo + tq - 1) // CHUNK + 1) * CHUNK
    nkb = jnp.minimum(nk, (k_hi + tk - 1) // tk)
    n_free = jnp.minimum(jnp.minimum(((q_lo // CHUNK + 1) * CHUNK) // tk, tk_valid // tk), nkb)
    acc_ref[...] = jnp.zeros_like(acc_ref)
    l_ref[...] = jnp.zeros_like(l_ref)
    m_ref[...] = jnp.full_like(m_ref, NEG_INF)
    q_chunk = (lax.broadcasted_iota(jnp.int32, (tk, tq), 1) + q_lo) // CHUNK
    krow = lax.broadcasted_iota(jnp.int32, (tk, tq), 0)
    c2 = (MLA_NOPE_DIM + MLA_ROPE_DIM) ** -0.5 * LOG2_E
    qs = (qa_ref[0], qb_ref[0])
    k_refs = (ka_ref, kb_ref)

    def blocks(kbs, masked):
        k0s = [pl.multiple_of(kb * tk, tk) for kb in kbs]
        raws = [[lax.dot_general(k_refs[h][0, pl.ds(k0, tk), :], qs[h], (((1,), (1,)), ((), ())),
                                 preferred_element_type=F32) for h in range(2)] for k0 in k0s]
        for k0, raw2 in zip(k0s, raws):
            v = v_ref[0, pl.ds(k0, tk), :]
            if masked:
                kidx = krow + k0
                mask = jnp.where(kidx < tk_valid, kidx // CHUNK, q_chunk + 1) <= q_chunk
            for h in range(2):
                raw = jnp.where(mask, raw2[h], NEG_INF) if masked else raw2[h]
                m_prev = m_ref[h, 0:1, :]
                m_next = jnp.maximum(m_prev, jnp.max(raw, axis=0, keepdims=True))
                p = jnp.exp2((raw - m_next) * c2)
                alpha = jnp.exp2((m_prev - m_next) * c2)
                l_next = alpha * l_ref[h, 0:1, :] + jnp.sum(p, axis=0, keepdims=True)
                pv = lax.dot_general(v, p.astype(BF16), (((0,), (0,)), ((), ())), preferred_element_type=F32)
                acc_ref[h] = alpha * acc_ref[h] + pv
                l_ref[h] = jnp.broadcast_to(l_next, (8, tq))
                m_ref[h] = jnp.broadcast_to(m_next, (8, tq))

    n4 = n_free // 4
    _loop(n4, lambda i: blocks(tuple(4 * i + d for d in range(4)), False))
    _loop((n_free % 4) // 2, lambda _: blocks((4 * n4, 4 * n4 + 1), False))
    _loop(n_free % 2, lambda _: blocks((n_free - 1,), False))
    n_edge = nkb - n_free
    _loop(n_edge // 2, lambda i: blocks((n_free + 2 * i, n_free + 2 * i + 1), True))
    _loop(n_edge % 2, lambda _: blocks((nkb - 1,), True))
    sub = lax.broadcasted_iota(jnp.int32, (LANES, tq), 0)
    out_t = jnp.where(sub < MLA_V_DIM, acc_ref[0] / l_ref[0, 0:1, :], acc_ref[1] / l_ref[1, 0:1, :])
    o_ref[0] = out_t.T.astype(BF16)


def _mla_attention(q, k, v, q_off, tk_valid, tq, tk):
    b, t_q, _ = q.shape
    t_k = k.shape[1]
    nk = t_k // tk
    kern = functools.partial(_mla_kernel, tq=tq, tk=tk, nk=nk, q_off=q_off, tk_valid=tk_valid)
    q_spec = lambda o: pl.BlockSpec((1, tq, LANES), lambda bi, p, qi: (bi, qi, 2 * p + o))
    k_spec = lambda o: pl.BlockSpec((1, t_k, LANES), lambda bi, p, qi: (bi, 0, 2 * p + o))
    return pl.pallas_call(
        kern,
        grid=(b, MLA_HEADS // 2, t_q // tq),
        in_specs=[q_spec(0), q_spec(1), k_spec(0), k_spec(1),
                  pl.BlockSpec((1, t_k, LANES), lambda bi, p, qi: (bi, 0, p))],
        out_specs=pl.BlockSpec((1, tq, LANES), lambda bi, p, qi: (bi, qi, p)),
        out_shape=jax.ShapeDtypeStruct((b, t_q, MLA_WIDTH), BF16),
        scratch_shapes=[pltpu.VMEM((2, LANES, tq), F32), pltpu.VMEM((2, 8, tq), F32), pltpu.VMEM((2, 8, tq), F32)],
        compiler_params=_cparams(("parallel", "parallel", "arbitrary")),
        name="mla_attention",
    )(q, q, k, k, v)


def _merge_kernel(x_ref, oa_ref, ob_ref, wga_ref, wgb_ref, woa_ref, wob_ref, wout_ref, g_ref, b_ref, h_o, *, alpha):
    x = x_ref[...]
    xb = x.astype(BF16)
    ga = jax.nn.sigmoid(jnp.dot(xb, wga_ref[...], preferred_element_type=F32))
    gb = jax.nn.sigmoid(jnp.dot(xb, wgb_ref[...], preferred_element_type=F32))
    merged = (ga * jnp.dot(oa_ref[...], woa_ref[...], preferred_element_type=F32)
              + gb * jnp.dot(ob_ref[...], wob_ref[...], preferred_element_type=F32))
    y = alpha * x + jnp.dot(merged.astype(BF16), wout_ref[...], preferred_element_type=F32)
    h_o[...] = _layer_norm(y, g_ref[...], b_ref[...])


def _merge(x, oa, ob, w, tm, alpha):
    n = x.shape[0]
    tok = lambda width: pl.BlockSpec((tm, width), lambda i: (i, 0))
    weights = (w["w_ga"], w["w_gb"], w["w_oa"], w["w_ob"], w["w_out"], w["ln1_g"], w["ln1_b"])
    return pl.pallas_call(
        functools.partial(_merge_kernel, alpha=alpha),
        grid=(n // tm,),
        in_specs=[tok(D_MODEL), tok(SB_WIDTH), tok(MLA_WIDTH)] + [_full(a.shape) for a in weights],
        out_specs=tok(D_MODEL),
        out_shape=jax.ShapeDtypeStruct((n, D_MODEL), F32),
        compiler_params=_cparams(("parallel",)),
        name="merge",
    )(x, oa, ob, *weights)


PEER_TE = 1024
I1_PER_TILE = PEER_TE // PEER_N_KEYS
W_CHUNK = 512
CAND_COUNTS = tuple(PEER_TOPK // (a + 1) for a in range(PEER_TOPK))
CAND_STARTS = tuple(int(x) for x in np.cumsum((0,) + CAND_COUNTS[:-1]))
N_CAND = sum(CAND_COUNTS)
CAND_ROWS = -(-N_CAND // 8) * 8


def _extract_rounds(s_refs, n_rows, tb):
    row = lax.broadcasted_iota(jnp.int32, (n_rows, tb), 0).astype(F32)
    slot = lax.broadcasted_iota(jnp.int32, (PEER_TOPK, tb), 0)

    def body(r, carry):
        out = []
        for s_ref, (vals, firsts) in zip(s_refs, carry):
            s = s_ref[...]
            m = jnp.max(s, axis=0, keepdims=True)
            first = jnp.min(jnp.where(s == m, row, float(n_rows)), axis=0, keepdims=True)
            s_ref[...] = jnp.where(row == first, -jnp.inf, s)
            out.append((jnp.where(slot == r, m, vals), jnp.where(slot == r, first, firsts)))
        return tuple(out)

    init = tuple((jnp.zeros((PEER_TOPK, tb), F32), jnp.zeros((PEER_TOPK, tb), F32)) for _ in s_refs)
    return lax.fori_loop(0, PEER_TOPK, body, init)


def _remove_maxima(s_refs, tb):
    slot = lax.broadcasted_iota(jnp.int32, (PEER_TOPK, tb), 0)

    def body(r, carry):
        out = []
        for s_ref, vals in zip(s_refs, carry):
            s = s_ref[...]
            m = jnp.max(s, axis=0, keepdims=True)
            s_ref[...] = jnp.where(s == m, -jnp.inf, s)
            out.append(jnp.where(slot == r, m, vals))
        return tuple(out)

    return lax.fori_loop(0, PEER_TOPK, body, tuple(jnp.zeros((PEER_TOPK, tb), F32) for _ in s_refs))


def _removed_mismatch(s_after, expected):
    gone = jnp.where(s_after == -jnp.inf, 1.0, 0.0)
    per_col = jnp.sum(gone.reshape(-1, gone.shape[-1]), axis=0, keepdims=True)
    return (jnp.max(jnp.abs(per_col - float(expected))) > 0.0).astype(jnp.int32)


def _scatter_rows(firsts, values, n_rows, tb, fill):
    row = lax.broadcasted_iota(jnp.int32, (n_rows, tb), 0).astype(F32)
    out = jnp.full((n_rows, tb), fill, F32)
    for r in range(PEER_TOPK):
        out = jnp.where(row == firsts[r:r + 1, :], values[r], out)
    return out


def _peer_kernel(h_ref, wqt_ref, keys_ref, u_ref, vt_ref, g_ref, b_ref, y_o,
                 htb_ref, qt_ref, n1_ref, e1_ref, rank2_ref, e2_ref, work_ref, cand_ref, a0_ref, a1_ref, outt_ref,
                 *, tb, alpha):
    j = pl.program_id(1)

    @pl.when(j == 0)
    def _prologue():
        htb = h_ref[...].T.astype(BF16)
        htb_ref[...] = htb
        qt_ref[...] = jnp.dot(wqt_ref[...], htb, preferred_element_type=F32)
        outt_ref[...] = jnp.zeros_like(outt_ref)
        a1_ref[...] = jnp.zeros_like(a1_ref)

        def fill_candidates(t1, t2):
            cand_ref[...] = jnp.full_like(cand_ref, -jnp.inf)
            for a in range(PEER_TOPK):
                cand_ref[CAND_STARTS[a]:CAND_STARTS[a] + CAND_COUNTS[a], :] = t1[a:a + 1, :] + t2[0:CAND_COUNTS[a], :]

        def finish(h, s, t1, t2, top, n1, rank2):
            n1_ref[h] = n1
            rank2_ref[h] = rank2.astype(BF16)
            z = jnp.sum(jnp.exp(top - top[0:1, :]), axis=0, keepdims=True)
            e1_ref[h] = jnp.exp(s[0] - t1[0:1, :])
            e2_ref[h] = (jnp.exp(s[1] - t2[0:1, :]) / z).astype(BF16)

        def head_by_value(h, s):
            work_ref[0] = s[0]
            work_ref[1] = s[1]
            t1, t2 = _remove_maxima((work_ref.at[0], work_ref.at[1]), tb)
            bad = _removed_mismatch(work_ref[...], 2 * PEER_TOPK)
            fill_candidates(t1, t2)
            (top,) = _remove_maxima((cand_ref,), tb)
            gone = cand_ref[...] == -jnp.inf
            bad = jnp.maximum(bad, _removed_mismatch(cand_ref[...], PEER_TOPK + CAND_ROWS - N_CAND))
            n1 = jnp.zeros((PEER_N_KEYS, tb), F32)
            rank2 = jnp.full((PEER_N_KEYS, tb), float(PEER_TOPK), F32)
            for a in range(PEER_TOPK):
                cnt = jnp.sum(jnp.where(gone[CAND_STARTS[a]:CAND_STARTS[a] + CAND_COUNTS[a], :], 1.0, 0.0),
                              axis=0, keepdims=True)
                n1 = jnp.where(s[0] == t1[a:a + 1, :], cnt, n1)
                rank2 = jnp.where(s[1] == t2[a:a + 1, :], float(a), rank2)
            finish(h, s, t1, t2, top, n1, rank2)
            return bad

        def head_exact(h, s):
            work_ref[0] = s[0]
            work_ref[1] = s[1]
            (t1, first1), (t2, first2) = _extract_rounds((work_ref.at[0], work_ref.at[1]), PEER_N_KEYS, tb)
            fill_candidates(t1, t2)
            ((top, cfirst),) = _extract_rounds((cand_ref,), CAND_ROWS, tb)
            counts = [jnp.sum(jnp.where(cfirst >= float(CAND_STARTS[a]),
                                        jnp.where(cfirst < float(CAND_STARTS[a] + CAND_COUNTS[a]), 1.0, 0.0), 0.0),
                              axis=0, keepdims=True) for a in range(PEER_TOPK)]
            n1 = _scatter_rows(first1, counts, PEER_N_KEYS, tb, 0.0)
            rank2 = _scatter_rows(first2, [float(r) for r in range(PEER_TOPK)], PEER_N_KEYS, tb, float(PEER_TOPK))
            finish(h, s, t1, t2, top, n1, rank2)

        def per_head(h, carry):
            s = []
            for c in range(2):
                r0 = pl.multiple_of((2 * h + c) * PEER_HALF, PEER_HALF)
                q = qt_ref[pl.ds(r0, PEER_HALF), :].astype(BF16)
                s.append(jnp.dot(keys_ref[2 * h + c], q, preferred_element_type=F32))
            tie = head_by_value(h, s)

            @pl.when(tie > 0)
            def _():
                head_exact(h, s)

            return carry

        lax.fori_loop(0, PEER_HEADS, per_head, 0)

    i1_0 = pl.multiple_of(jnp.maximum(j - 1, 0) * I1_PER_TILE, I1_PER_TILE)

    def step(a_next_ref, a_prev_ref):
        a_next_ref[...] = jnp.dot(u_ref[...], htb_ref[...], preferred_element_type=F32)
        n1_rows = [n1_ref[h, pl.ds(i1_0, I1_PER_TILE), :] for h in range(PEER_HEADS)]
        e1_rows = [e1_ref[h, pl.ds(i1_0, I1_PER_TILE), :] for h in range(PEER_HEADS)]
        for c in range(PEER_TE // W_CHUNK):
            w = []
            for i in range(c * W_CHUNK // PEER_N_KEYS, (c + 1) * W_CHUNK // PEER_N_KEYS):
                a = a_prev_ref[i * PEER_N_KEYS:(i + 1) * PEER_N_KEYS, :]
                act = 0.5 * a * (1.0 + lax.erf(a * (0.5 ** 0.5)))
                gate = jnp.zeros((PEER_N_KEYS, tb), BF16)
                for h in range(PEER_HEADS):
                    n1b = jnp.broadcast_to(n1_rows[h][i:i + 1, :], (PEER_N_KEYS, tb)).astype(BF16)
                    e1b = jnp.broadcast_to(e1_rows[h][i:i + 1, :], (PEER_N_KEYS, tb)).astype(BF16)
                    gate = gate + jnp.where(rank2_ref[h] < n1b, e1b * e2_ref[h], jnp.zeros_like(gate))
                w.append(gate * act.astype(BF16))
            outt_ref[...] += jnp.dot(vt_ref[:, c * W_CHUNK:(c + 1) * W_CHUNK], jnp.concatenate(w, axis=0),
                                     preferred_element_type=F32)

    @pl.when(j % 2 == 0)
    def _even():
        step(a0_ref, a1_ref)

    @pl.when(j % 2 == 1)
    def _odd():
        step(a1_ref, a0_ref)

    @pl.when(j == pl.num_programs(1) - 1)
    def _epilogue():
        h = h_ref[...]
        y_o[...] = _layer_norm(alpha * h + outt_ref[...].T, g_ref[...], b_ref[...])


def _peer(h, w, tb, alpha):
    n = h.shape[0]
    n_tiles = w["u"].shape[0] // PEER_TE
    weights_pre = (w["wq_t"], w["keys"])
    return pl.pallas_call(
        functools.partial(_peer_kernel, tb=tb, alpha=alpha),
        grid=(n // tb, n_tiles + 1),
        in_specs=[pl.BlockSpec((tb, D_MODEL), lambda i, j: (i, 0))]
        + [_full(a.shape) for a in weights_pre]
        + [pl.BlockSpec((PEER_TE, D_MODEL), lambda i, j: (jnp.minimum(j, n_tiles - 1), 0)),
           pl.BlockSpec((D_MODEL, PEER_TE), lambda i, j: (0, jnp.maximum(j - 1, 0))),
           _full(w["ln2_g"].shape), _full(w["ln2_b"].shape)],
        out_specs=pl.BlockSpec((tb, D_MODEL), lambda i, j: (i, 0)),
        out_shape=jax.ShapeDtypeStruct((n, D_MODEL), F32),
        scratch_shapes=[
            pltpu.VMEM((D_MODEL, tb), BF16),
            pltpu.VMEM((2 * PEER_HEADS * PEER_HALF, tb), F32),
            pltpu.VMEM((PEER_HEADS, PEER_N_KEYS, tb), F32),
            pltpu.VMEM((PEER_HEADS, PEER_N_KEYS, tb), F32),
            pltpu.VMEM((PEER_HEADS, PEER_N_KEYS, tb), BF16),
            pltpu.VMEM((PEER_HEADS, PEER_N_KEYS, tb), BF16),
            pltpu.VMEM((2, PEER_N_KEYS, tb), F32),
            pltpu.VMEM((CAND_ROWS, tb), F32),
            pltpu.VMEM((PEER_TE, tb), F32),
            pltpu.VMEM((PEER_TE, tb), F32),
            pltpu.VMEM((D_MODEL, tb), F32),
        ],
        compiler_params=_cparams(("parallel", "arbitrary")),
        name="peer",
    )(h, *weights_pre, w["u"], w["v_t"], w["ln2_g"], w["ln2_b"])


def _rope_tables(pos):
    inv = ROPE_THETA ** (-jnp.arange(ROPE_HALF, dtype=F32) / ROPE_HALF)
    ang = pos.astype(F32)[:, None] * inv[None, :]
    cos, sin = jnp.cos(ang), jnp.sin(ang)
    n = pos.shape[0]
    z16 = jnp.zeros((n, ROPE_HALF), F32)
    head = lambda fill: jnp.full((n, MLA_NOPE_DIM), fill, F32)
    tail = jnp.zeros((n, LANES - MLA_NOPE_DIM - MLA_ROPE_DIM), F32)
    cq = jnp.concatenate([head(1.0), cos, cos, tail], axis=1)
    ck = jnp.concatenate([head(0.0), cos, cos, tail], axis=1)
    s1 = jnp.concatenate([head(0.0), -sin, z16, tail], axis=1)
    s2 = jnp.concatenate([head(0.0), z16, sin, tail], axis=1)
    return cq, ck, s1, s2


def _pad_heads(wm, width):
    k = wm.shape[0]
    wm = wm.reshape(k, MLA_HEADS, width)
    return jnp.pad(wm, ((0, 0), (0, 0), (0, HEAD_PAD - width))).reshape(k, MLA_HEADS * HEAD_PAD)


def _prep_weights(w_in, q_norm_g, w_uq, kv_norm_g, w_uk, w_uv, w_oa, w_ob, w_out, ln1_g, ln1_b,
                  peer_wq, peer_sub_keys, peer_u, peer_v, ln2_g, ln2_b):
    o = 3 * SB_WIDTH + MLA_Q_LORA + MLA_KV_LORA
    w_kr = jnp.pad(w_in[:, o:o + MLA_ROPE_DIM], ((0, 0), (MLA_NOPE_DIM, LANES - MLA_NOPE_DIM - MLA_ROPE_DIM)))
    place = np.zeros((MLA_ROPE_DIM, LANES), np.float32)
    place[np.arange(MLA_ROPE_DIM), MLA_NOPE_DIM + np.arange(MLA_ROPE_DIM)] = 1.0
    og = o + MLA_ROPE_DIM
    row = lambda a: a.reshape(1, -1).astype(F32)
    return {
        "w_all": jnp.concatenate([w_in[:, :o], w_kr], axis=1).astype(BF16),
        "q_g": row(q_norm_g), "kv_g": row(kv_norm_g),
        "w_uq": _pad_heads(w_uq, MLA_NOPE_DIM + MLA_ROPE_DIM).astype(BF16),
        "w_uk": _pad_heads(w_uk, MLA_NOPE_DIM).astype(BF16),
        "w_uv": w_uv.astype(BF16),
        "place": jnp.asarray(place, dtype=BF16),
        "w_ga": w_in[:, og:og + D_MODEL].astype(BF16), "w_gb": w_in[:, og + D_MODEL:og + 2 * D_MODEL].astype(BF16),
        "w_oa": w_oa.astype(BF16), "w_ob": w_ob.astype(BF16), "w_out": w_out.astype(BF16),
        "ln1_g": row(ln1_g), "ln1_b": row(ln1_b), "ln2_g": row(ln2_g), "ln2_b": row(ln2_b),
        "wq_t": peer_wq.T.astype(BF16),
        "keys": peer_sub_keys.reshape(2 * PEER_HEADS, PEER_N_KEYS, PEER_HALF).astype(BF16),
        "u": peer_u.astype(BF16),
        "v_t": peer_v.T.astype(BF16),
    }


def _tile(n, pref):
    return pref if n % pref == 0 else n


def _layer(x, pos, past, w, alpha):
    b, t, _ = x.shape
    n = b * t
    tm = _tile(t, 512)
    xf = x.reshape(n, D_MODEL)
    sbq, sbk, sbv, sbkb, sbvb, ckv, kpe, qcat, kcat, vm = _proj(xf, _rope_tables(pos), w, tm)
    r3 = lambda a: a.reshape(b, t, a.shape[-1])
    q_sb, k_sb, v_sb, q_m, k_m, v_m = map(r3, (sbq, sbkb, sbvb, qcat, kcat, vm))
    if past is None:
        q_off, tk_valid = 0, t
    else:
        p_len = past[0].shape[1]
        flat = lambda a: a.reshape((b * p_len,) + a.shape[2:])
        pk, pv, pkc, pvm = _cache_prep(flat(past[0]), flat(past[1]), flat(past[2]), flat(past[3]), w,
                                       _tile(b * p_len, 1024))
        p3 = lambda a: a.reshape(b, p_len, a.shape[-1])
        k_sb = jnp.concatenate([p3(pk), k_sb], axis=1)
        v_sb = jnp.concatenate([p3(pv), v_sb], axis=1)
        k_m = jnp.concatenate([p3(pkc), k_m], axis=1)
        v_m = jnp.concatenate([p3(pvm), v_m], axis=1)
        q_off, tk_valid = p_len, p_len + t
    tq = _tile(t, 512)
    tk = 256
    pad = (-k_sb.shape[1]) % tk
    if pad:
        padk = lambda a: jnp.pad(a, ((0, 0), (0, pad), (0, 0)))
        k_sb, v_sb, k_m, v_m = map(padk, (k_sb, v_sb, k_m, v_m))
    o_a = _sb_attention(q_sb, k_sb, v_sb, q_off, tq, tk)
    o_b = _mla_attention(q_m, k_m, v_m, q_off, tk_valid, tq, tk)
    h = _merge(xf, o_a.reshape(n, SB_WIDTH), o_b.reshape(n, MLA_WIDTH), w, tm, alpha)
    y = _peer(h, w, _tile(n, 512), alpha)
    state = (sbk.reshape(b, t, SB_HEADS, SB_HEAD_DIM), sbv.reshape(b, t, SB_HEADS, SB_HEAD_DIM),
             ckv.reshape(b, t, MLA_KV_LORA), kpe.reshape(b, t, MLA_ROPE_DIM))
    return y.reshape(b, t, D_MODEL), state


def kernel(x_prompt, x_sample, cache_sb_k, cache_sb_v, cache_mla_ckv, cache_mla_kpe, w_in, q_norm_g, w_uq,
           kv_norm_g, w_uk, w_uv, w_oa, w_ob, w_out, ln1_g, ln1_b, peer_wq, peer_sub_keys, peer_u, peer_v,
           ln2_g, ln2_b):
    depth = w_in.shape[0]
    alpha = (2.0 * depth) ** 0.25
    past_len = cache_sb_k.shape[2]
    pos_p = jnp.arange(x_prompt.shape[1], dtype=jnp.int32)
    pos_s = past_len + jnp.arange(x_sample.shape[1], dtype=jnp.int32)
    y_p, y_s = x_prompt, x_sample
    st_p, st_s = [], []
    for l in range(depth):
        w = _prep_weights(w_in[l], q_norm_g[l], w_uq[l], kv_norm_g[l], w_uk[l], w_uv[l], w_oa[l], w_ob[l],
                          w_out[l], ln1_g[l], ln1_b[l], peer_wq[l], peer_sub_keys[l], peer_u[l], peer_v[l],
                          ln2_g[l], ln2_b[l])
        y_p, sp = _layer(y_p, pos_p, None, w, alpha)
        y_s, ss = _layer(y_s, pos_s, (cache_sb_k[l], cache_sb_v[l], cache_mla_ckv[l], cache_mla_kpe[l]), w, alpha)
        st_p.append(sp)
        st_s.append(ss)
    stack = lambda sts, i: jnp.stack([s[i] for s in sts])
    return (y_p, y_s, stack(st_p, 0), stack(st_p, 1), stack(st_p, 2), stack(st_p, 3),
            stack(st_s, 0), stack(st_s, 1), stack(st_s, 2), stack(st_s, 3))
```

```python
import functools

import numpy as np
import jax
import jax.numpy as jnp
from jax import lax
from jax.experimental import pallas as pl
from jax.experimental.pallas import tpu as pltpu

F32 = jnp.float32
BF16 = jnp.bfloat16

D_MODEL = 1024
CHUNK = 64
SB_HEADS = 8
SB_HEAD_DIM = 64
SB_WIDTH = SB_HEADS * SB_HEAD_DIM
MLA_HEADS = 8
MLA_NOPE_DIM = 64
MLA_ROPE_DIM = 32
MLA_V_DIM = 64
MLA_Q_LORA = 384
MLA_KV_LORA = 256
MLA_WIDTH = MLA_HEADS * MLA_V_DIM
ROPE_THETA = 10000.0
PEER_HEADS = 8
PEER_N_KEYS = 128
PEER_TOPK = 16
PEER_HALF = 128
EPS = 1e-5
NEG_INF = -1e30
LOG2_E = 1.4426950408889634
EXP_ZERO_F32 = -104.0

LANES = 128
HEAD_PAD = LANES
ROPE_HALF = MLA_ROPE_DIM // 2
VMEM_LIMIT = 56 * 2**20


def _cparams(sem):
    return pltpu.CompilerParams(dimension_semantics=sem, vmem_limit_bytes=VMEM_LIMIT)


def _full(shape):
    nd = len(shape)
    return pl.BlockSpec(shape, lambda *_: (0,) * nd)


def _rms(x, g):
    ms = jnp.mean(x * x, axis=-1, keepdims=True)
    return x * lax.rsqrt(ms + EPS) * g


def _layer_norm(x, g, b):
    mu = jnp.mean(x, axis=-1, keepdims=True)
    xc = x - mu
    var = jnp.mean(xc * xc, axis=-1, keepdims=True)
    return xc * lax.rsqrt(var + EPS) * g + b


def _rope_lanes(x, c, s1, s2):
    return x * c + pltpu.roll(x, LANES - ROPE_HALF, 1) * s1 + pltpu.roll(x, ROPE_HALF, 1) * s2


W_ALL = 3 * SB_WIDTH + MLA_Q_LORA + MLA_KV_LORA + LANES


def _proj_kernel(x_ref, cq_ref, ck_ref, s1_ref, s2_ref, wall_ref, qg_ref, kvg_ref, wuq_ref, wuk_ref, wuv_ref,
                 sbq_o, sbk_o, sbv_o, sbkb_o, sbvb_o, ckv_o, kpe_o, qcat_o, kcat_o, vm_o):
    xb = x_ref[...].astype(BF16)
    proj = jnp.dot(xb, wall_ref[...], preferred_element_type=F32)
    sbq_o[...] = (proj[:, 0:SB_WIDTH] * (SB_HEAD_DIM ** -0.5)).astype(BF16)
    k = proj[:, SB_WIDTH:2 * SB_WIDTH]
    v = proj[:, 2 * SB_WIDTH:3 * SB_WIDTH]
    for h in range(SB_HEADS):
        sl = slice(h * SB_HEAD_DIM, (h + 1) * SB_HEAD_DIM)
        sbk_o[:, h, :] = k[:, sl]
        sbv_o[:, h, :] = v[:, sl]
    sbkb_o[...] = k.astype(BF16)
    sbvb_o[...] = v.astype(BF16)
    o = 3 * SB_WIDTH
    cqn = _rms(proj[:, o:o + MLA_Q_LORA], qg_ref[...])
    q = jnp.dot(cqn.astype(BF16), wuq_ref[...], preferred_element_type=F32)
    o += MLA_Q_LORA
    ckv = _rms(proj[:, o:o + MLA_KV_LORA], kvg_ref[...])
    ckv_o[...] = ckv
    o += MLA_KV_LORA
    cb = ckv.astype(BF16)
    kn = jnp.dot(cb, wuk_ref[...], preferred_element_type=F32)
    vm_o[...] = jnp.dot(cb, wuv_ref[...], preferred_element_type=F32).astype(BF16)
    s1 = s1_ref[...]
    s2 = s2_ref[...]
    kp = _rope_lanes(proj[:, o:o + LANES], ck_ref[...], s1, s2)
    kpe_o[...] = kp[:, MLA_NOPE_DIM:MLA_NOPE_DIM + MLA_ROPE_DIM]
    cq = cq_ref[...]
    for h in range(MLA_HEADS):
        sl = slice(h * HEAD_PAD, (h + 1) * HEAD_PAD)
        qcat_o[:, sl] = _rope_lanes(q[:, sl], cq, s1, s2).astype(BF16)
        kcat_o[:, sl] = (kn[:, sl] + kp).astype(BF16)


def _proj(x, tabs, w, tm):
    n = x.shape[0]
    rt = tabs[0].shape[0] // tm
    tok = lambda width: pl.BlockSpec((tm, width), lambda i: (i, 0))
    tab = pl.BlockSpec((tm, LANES), lambda i: (i % rt, 0))
    outs = [(SB_WIDTH, BF16), (SB_WIDTH, F32), (SB_WIDTH, F32), (SB_WIDTH, BF16), (SB_WIDTH, BF16),
            (MLA_KV_LORA, F32), (MLA_ROPE_DIM, F32), (MLA_HEADS * HEAD_PAD, BF16), (MLA_HEADS * HEAD_PAD, BF16),
            (MLA_WIDTH, BF16)]
    weights = (w["w_all"], w["q_g"], w["kv_g"], w["w_uq"], w["w_uk"], w["w_uv"])
    out_specs = [tok(wd) for wd, _ in outs]
    out_shape = [jax.ShapeDtypeStruct((n, wd), dt) for wd, dt in outs]
    for i in (1, 2):
        out_specs[i] = pl.BlockSpec((tm, SB_HEADS, SB_HEAD_DIM), lambda i: (i, 0, 0))
        out_shape[i] = jax.ShapeDtypeStruct((n, SB_HEADS, SB_HEAD_DIM), F32)
    return pl.pallas_call(
        _proj_kernel,
        grid=(n // tm,),
        in_specs=[tok(D_MODEL), tab, tab, tab, tab] + [_full(a.shape) for a in weights],
        out_specs=out_specs,
        out_shape=out_shape,
        compiler_params=_cparams(("parallel",)),
        name="proj",
    )(x, *tabs, *weights)


def _cache_kernel(ckv_ref, kpe_ref, wuk_ref, wuv_ref, place_ref, kcat_o, vm_o):
    cb = ckv_ref[...].astype(BF16)
    kn = jnp.dot(cb, wuk_ref[...], preferred_element_type=F32)
    vm_o[...] = jnp.dot(cb, wuv_ref[...], preferred_element_type=F32).astype(BF16)
    kp = jnp.dot(kpe_ref[...].astype(BF16), place_ref[...], preferred_element_type=F32)
    for h in range(MLA_HEADS):
        sl = slice(h * HEAD_PAD, (h + 1) * HEAD_PAD)
        kcat_o[:, sl] = (kn[:, sl] + kp).astype(BF16)


def _cache_prep(ckv, kpe, w, tm):
    n = ckv.shape[0]
    tok = lambda width: pl.BlockSpec((tm, width), lambda i: (i, 0))
    outs = [(MLA_HEADS * HEAD_PAD, BF16), (MLA_WIDTH, BF16)]
    weights = (w["w_uk"], w["w_uv"], w["place"])
    return pl.pallas_call(
        _cache_kernel,
        grid=(n // tm,),
        in_specs=[tok(MLA_KV_LORA), tok(MLA_ROPE_DIM)] + [_full(a.shape) for a in weights],
        out_specs=[tok(wd) for wd, _ in outs],
        out_shape=[jax.ShapeDtypeStruct((n, wd), dt) for wd, dt in outs],
        compiler_params=_cparams(("parallel",)),
        name="cache_prep",
    )(ckv, kpe, *weights)


def _loop(n, step):
    def one(i, carry):
        step(i)
        return carry

    lax.fori_loop(0, n, one, 0)


def _sb_kernel(q_ref, k_ref, v_ref, tri_ref, o_ref, acc_ref, c_ref, *, tq, tk, nk, q_off):
    qi = pl.program_id(2)
    q = q_ref[0]
    lane = lax.broadcasted_iota(jnp.int32, (tq, LANES), 1)
    zero = jnp.zeros_like(q)
    qs = (jnp.where(lane < SB_HEAD_DIM, q, zero), jnp.where(lane >= SB_HEAD_DIM, q, zero))
    q_lo = q_off + qi * tq
    nkb = jnp.minimum(nk, (q_lo + tq - 1 + tk - 1) // tk)
    kb_diag = jnp.minimum(q_lo // tk, nkb)
    acc_ref[...] = jnp.zeros_like(acc_ref)
    c_ref[...] = jnp.zeros_like(c_ref)
    q_pos = lax.broadcasted_iota(jnp.int32, (tk, tq), 1) + q_lo
    krow = lax.broadcasted_iota(jnp.int32, (tk, tq), 0)
    tri = tri_ref[...]

    def blocks(kbs, masked):
        k0s = [pl.multiple_of(kb * tk, tk) for kb in kbs]
        zs = [[lax.dot_general(k_ref[0, pl.ds(k0, tk), :], qs[h], (((1,), (1,)), ((), ())),
                               preferred_element_type=F32) for h in range(2)] for k0 in k0s]
        stage = []
        for k0, z2 in zip(k0s, zs):
            causal = (krow + k0) < q_pos if masked else None
            for z in z2:
                lb = jnp.minimum(z, 0.0) - jnp.log(1.0 + jnp.exp(-jnp.abs(z)))
                lom = lb - z
                if masked:
                    lom = jnp.where(causal, lom, 0.0)
                hi = lom.astype(BF16)
                lo = (lom - hi.astype(F32)).astype(BF16)
                suf = jnp.dot(tri, jnp.concatenate([hi, lo], axis=0), preferred_element_type=F32)
                stage.append((lb, suf, suf[0:1, :] + lom[0:1, :], causal))
        for b, k0 in enumerate(k0s):
            v = v_ref[0, pl.ds(k0, tk), :]
            for h in range(2):
                lb, suf, tot, causal = stage[2 * b + h]
                c = c_ref[h, 0:1, :]
                a = jnp.exp(lb + suf + c)
                if masked:
                    a = jnp.where(causal, a, 0.0)
                acc_ref[h] += lax.dot_general(v, a.astype(BF16), (((0,), (0,)), ((), ())),
                                              preferred_element_type=F32)
                c_ref[h] = jnp.broadcast_to(c + tot, (8, tq))

    n_diag = nkb - kb_diag
    _loop(n_diag // 2, lambda i: blocks((nkb - 1 - 2 * i, nkb - 2 - 2 * i), True))
    _loop(n_diag % 2, lambda _: blocks((kb_diag,), True))

    def live():
        return (jnp.max(c_ref[...]) > EXP_ZERO_F32).astype(jnp.int32)

    def more(state):
        return jnp.logical_and(state[0] < kb_diag, state[1] > 0)

    def back(state):
        blocks((kb_diag - 1 - state[0],), False)
        return state[0] + 1, live()

    lax.while_loop(more, back, (jnp.int32(0), live()))
    sub = lax.broadcasted_iota(jnp.int32, (LANES, tq), 0)
    o_ref[0] = jnp.where(sub < SB_HEAD_DIM, acc_ref[0], acc_ref[1]).T.astype(BF16)


def _suffix_matrix(tk):
    s = np.arange(tk)[:, None]
    j = np.arange(tk)[None, :]
    tri = (j > s).astype(np.float32)
    return jnp.asarray(np.concatenate([tri, tri], axis=1), dtype=BF16)


def _sb_attention(q, k, v, q_off, tq, tk):
    b, t_q, _ = q.shape
    t_k = k.shape[1]
    nk = t_k // tk
    kern = functools.partial(_sb_kernel, tq=tq, tk=tk, nk=nk, q_off=q_off)
    kv_spec = pl.BlockSpec((1, t_k, LANES), lambda bi, p, qi: (bi, 0, p))
    return pl.pallas_call(
        kern,
        grid=(b, SB_WIDTH // LANES, t_q // tq),
        in_specs=[pl.BlockSpec((1, tq, LANES), lambda bi, p, qi: (bi, qi, p)), kv_spec, kv_spec,
                  _full((tk, 2 * tk))],
        out_specs=pl.BlockSpec((1, tq, LANES), lambda bi, p, qi: (bi, qi, p)),
        out_shape=jax.ShapeDtypeStruct((b, t_q, SB_WIDTH), BF16),
        scratch_shapes=[pltpu.VMEM((2, LANES, tq), F32), pltpu.VMEM((2, 8, tq), F32)],
        compiler_params=_cparams(("parallel", "parallel", "arbitrary")),
        name="sb_attention",
    )(q, k, v, _suffix_matrix(tk))


def _mla_kernel(qa_ref, qb_ref, ka_ref, kb_ref, v_ref, o_ref, acc_ref, m_ref, l_ref, *, tq, tk, nk, q_off, tk_valid):
    qi = pl.program_id(2)
    q_lo = q_off + qi * tq
    k_hi = ((q_lo + tq - 1) // CHUNK + 1) * CHUNK
    nkb = jnp.minimum(nk, (k_hi + tk - 1) // tk)
    n_free = jnp.minimum(jnp.minimum(((q_lo // CHUNK + 1) * CHUNK) // tk, tk_valid // tk), nkb)
    acc_ref[...] = jnp.zeros_like(acc_ref)
    l_ref[...] = jnp.zeros_like(l_ref)
    m_ref[...] = jnp.full_like(m_ref, NEG_INF)
    q_chunk = (lax.broadcasted_iota(jnp.int32, (tk, tq), 1) + q_lo) // CHUNK
    krow = lax.broadcasted_iota(jnp.int32, (tk, tq), 0)
    c2 = (MLA_NOPE_DIM + MLA_ROPE_DIM) ** -0.5 * LOG2_E
    qs = (qa_ref[0], qb_ref[0])
    k_refs = (ka_ref, kb_ref)

    def blocks(kbs, masked):
        k0s = [pl.multiple_of(kb * tk, tk) for kb in kbs]
        raws = [[lax.dot_general(k_refs[h][0, pl.ds(k0, tk), :], qs[h], (((1,), (1,)), ((), ())),
                                 preferred_element_type=F32) for h in range(2)] for k0 in k0s]
        for k0, raw2 in zip(k0s, raws):
            v = v_ref[0, pl.ds(k0, tk), :]
            if masked:
                kidx = krow + k0
                mask = jnp.where(kidx < tk_valid, kidx // CHUNK, q_chunk + 1) <= q_chunk
            for h in range(2):
                raw = jnp.where(mask, raw2[h], NEG_INF) if masked else raw2[h]
                m_prev = m_ref[h, 0:1, :]
                m_next = jnp.maximum(m_prev, jnp.max(raw, axis=0, keepdims=True))
                p = jnp.exp2((raw - m_next) * c2)
                alpha = jnp.exp2((m_prev - m_next) * c2)
                l_next = alpha * l_ref[h, 0:1, :] + jnp.sum(p, axis=0, keepdims=True)
                pv = lax.dot_general(v, p.astype(BF16), (((0,), (0,)), ((), ())), preferred_element_type=F32)
                acc_ref[h] = alpha * acc_ref[h] + pv
                l_ref[h] = jnp.broadcast_to(l_next, (8, tq))
                m_ref[h] = jnp.broadcast_to(m_next, (8, tq))

    n4 = n_free // 4
    _loop(n4, lambda i: blocks(tuple(4 * i + d for d in range(4)), False))
    _loop((n_free % 4) // 2, lambda _: blocks((4 * n4, 4 * n4 + 1), False))
    _loop(n_free % 2, lambda _: blocks((n_free - 1,), False))
    n_edge = nkb - n_free
    _loop(n_edge // 2, lambda i: blocks((n_free + 2 * i, n_free + 2 * i + 1), True))
    _loop(n_edge % 2, lambda _: blocks((nkb - 1,), True))
    sub = lax.broadcasted_iota(jnp.int32, (LANES, tq), 0)
    out_t = jnp.where(sub < MLA_V_DIM, acc_ref[0] / l_ref[0, 0:1, :], acc_ref[1] / l_ref[1, 0:1, :])
    o_ref[0] = out_t.T.astype(BF16)


def _mla_attention(q, k, v, q_off, tk_valid, tq, tk):
    b, t_q, _ = q.shape
    t_k = k.shape[1]
    nk = t_k // tk
    kern = functools.partial(_mla_kernel, tq=tq, tk=tk, nk=nk, q_off=q_off, tk_valid=tk_valid)
    q_spec = lambda o: pl.BlockSpec((1, tq, LANES), lambda bi, p, qi: (bi, qi, 2 * p + o))
    k_spec = lambda o: pl.BlockSpec((1, t_k, LANES), lambda bi, p, qi: (bi, 0, 2 * p + o))
    return pl.pallas_call(
        kern,
        grid=(b, MLA_HEADS // 2, t_q // tq),
        in_specs=[q_spec(0), q_spec(1), k_spec(0), k_spec(1),
                  pl.BlockSpec((1, t_k, LANES), lambda bi, p, qi: (bi, 0, p))],
        out_specs=pl.BlockSpec((1, tq, LANES), lambda bi, p, qi: (bi, qi, p)),
        out_shape=jax.ShapeDtypeStruct((b, t_q, MLA_WIDTH), BF16),
        scratch_shapes=[pltpu.VMEM((2, LANES, tq), F32), pltpu.VMEM((2, 8, tq), F32), pltpu.VMEM((2, 8, tq), F32)],
        compiler_params=_cparams(("parallel", "parallel", "arbitrary")),
        name="mla_attention",
    )(q, q, k, k, v)


def _merge_kernel(x_ref, oa_ref, ob_ref, wga_ref, wgb_ref, woa_ref, wob_ref, wout_ref, g_ref, b_ref, h_o, *, alpha):
    x = x_ref[...]
    xb = x.astype(BF16)
    ga = jax.nn.sigmoid(jnp.dot(xb, wga_ref[...], preferred_element_type=F32))
    gb = jax.nn.sigmoid(jnp.dot(xb, wgb_ref[...], preferred_element_type=F32))
    merged = (ga * jnp.dot(oa_ref[...], woa_ref[...], preferred_element_type=F32)
              + gb * jnp.dot(ob_ref[...], wob_ref[...], preferred_element_type=F32))
    y = alpha * x + jnp.dot(merged.astype(BF16), wout_ref[...], preferred_element_type=F32)
    h_o[...] = _layer_norm(y, g_ref[...], b_ref[...])


def _merge(x, oa, ob, w, tm, alpha):
    n = x.shape[0]
    tok = lambda width: pl.BlockSpec((tm, width), lambda i: (i, 0))
    weights = (w["w_ga"], w["w_gb"], w["w_oa"], w["w_ob"], w["w_out"], w["ln1_g"], w["ln1_b"])
    return pl.pallas_call(
        functools.partial(_merge_kernel, alpha=alpha),
        grid=(n // tm,),
        in_specs=[tok(D_MODEL), tok(SB_WIDTH), tok(MLA_WIDTH)] + [_full(a.shape) for a in weights],
        out_specs=tok(D_MODEL),
        out_shape=jax.ShapeDtypeStruct((n, D_MODEL), F32),
        compiler_params=_cparams(("parallel",)),
        name="merge",
    )(x, oa, ob, *weights)


PEER_TE = 1024
I1_PER_TILE = PEER_TE // PEER_N_KEYS
W_CHUNK = 512
CAND_COUNTS = tuple(PEER_TOPK // (a + 1) for a in range(PEER_TOPK))
CAND_STARTS = tuple(int(x) for x in np.cumsum((0,) + CAND_COUNTS[:-1]))
N_CAND = sum(CAND_COUNTS)
CAND_ROWS = -(-N_CAND // 8) * 8


def _extract_rounds(s_refs, n_rows, tb):
    row = lax.broadcasted_iota(jnp.int32, (n_rows, tb), 0).astype(F32)
    slot = lax.broadcasted_iota(jnp.int32, (PEER_TOPK, tb), 0)

    def body(r, carry):
        out = []
        for s_ref, (vals, firsts) in zip(s_refs, carry):
            s = s_ref[...]
            m = jnp.max(s, axis=0, keepdims=True)
            first = jnp.min(jnp.where(s == m, row, float(n_rows)), axis=0, keepdims=True)
            s_ref[...] = jnp.where(row == first, -jnp.inf, s)
            out.append((jnp.where(slot == r, m, vals), jnp.where(slot == r, first, firsts)))
        return tuple(out)

    init = tuple((jnp.zeros((PEER_TOPK, tb), F32), jnp.zeros((PEER_TOPK, tb), F32)) for _ in s_refs)
    return lax.fori_loop(0, PEER_TOPK, body, init)


def _remove_maxima(s_refs, tb):
    slot = lax.broadcasted_iota(jnp.int32, (PEER_TOPK, tb), 0)

    def body(r, carry):
        out = []
        for s_ref, vals in zip(s_refs, carry):
            s = s_ref[...]
            m = jnp.max(s, axis=0, keepdims=True)
            s_ref[...] = jnp.where(s == m, -jnp.inf, s)
            out.append(jnp.where(slot == r, m, vals))
        return tuple(out)

    return lax.fori_loop(0, PEER_TOPK, body, tuple(jnp.zeros((PEER_TOPK, tb), F32) for _ in s_refs))


def _removed_mismatch(s_after, expected):
    gone = jnp.where(s_after == -jnp.inf, 1.0, 0.0)
    per_col = jnp.sum(gone.reshape(-1, gone.shape[-1]), axis=0, keepdims=True)
    return (jnp.max(jnp.abs(per_col - float(expected))) > 0.0).astype(jnp.int32)


def _scatter_rows(firsts, values, n_rows, tb, fill):
    row = lax.broadcasted_iota(jnp.int32, (n_rows, tb), 0).astype(F32)
    out = jnp.full((n_rows, tb), fill, F32)
    for r in range(PEER_TOPK):
        out = jnp.where(row == firsts[r:r + 1, :], values[r], out)
    return out


def _peer_kernel(h_ref, wqt_ref, keys_ref, u_ref, vt_ref, g_ref, b_ref, y_o,
                 htb_ref, qt_ref, n1_ref, e1_ref, rank2_ref, e2_ref, work_ref, cand_ref, a0_ref, a1_ref, outt_ref,
                 *, tb, alpha):
    j = pl.program_id(1)

    @pl.when(j == 0)
    def _prologue():
        htb = h_ref[...].T.astype(BF16)
        htb_ref[...] = htb
        qt_ref[...] = jnp.dot(wqt_ref[...], htb, preferred_element_type=F32)
        outt_ref[...] = jnp.zeros_like(outt_ref)
        a1_ref[...] = jnp.zeros_like(a1_ref)

        def fill_candidates(t1, t2):
            cand_ref[...] = jnp.full_like(cand_ref, -jnp.inf)
            for a in range(PEER_TOPK):
                cand_ref[CAND_STARTS[a]:CAND_STARTS[a] + CAND_COUNTS[a], :] = t1[a:a + 1, :] + t2[0:CAND_COUNTS[a], :]

        def finish(h, s, t1, t2, top, n1, rank2):
            n1_ref[h] = n1
            rank2_ref[h] = rank2.astype(BF16)
            z = jnp.sum(jnp.exp(top - top[0:1, :]), axis=0, keepdims=True)
            e1_ref[h] = jnp.exp(s[0] - t1[0:1, :])
            e2_ref[h] = (jnp.exp(s[1] - t2[0:1, :]) / z).astype(BF16)

        def head_by_value(h, s):
            work_ref[0] = s[0]
            work_ref[1] = s[1]
            t1, t2 = _remove_maxima((work_ref.at[0], work_ref.at[1]), tb)
            bad = _removed_mismatch(work_ref[...], 2 * PEER_TOPK)
            fill_candidates(t1, t2)
            (top,) = _remove_maxima((cand_ref,), tb)
            gone = cand_ref[...] == -jnp.inf
            bad = jnp.maximum(bad, _removed_mismatch(cand_ref[...], PEER_TOPK + CAND_ROWS - N_CAND))
            n1 = jnp.zeros((PEER_N_KEYS, tb), F32)
            rank2 = jnp.full((PEER_N_KEYS, tb), float(PEER_TOPK), F32)
            for a in range(PEER_TOPK):
                cnt = jnp.sum(jnp.where(gone[CAND_STARTS[a]:CAND_STARTS[a] + CAND_COUNTS[a], :], 1.0, 0.0),
                              axis=0, keepdims=True)
                n1 = jnp.where(s[0] == t1[a:a + 1, :], cnt, n1)
                rank2 = jnp.where(s[1] == t2[a:a + 1, :], float(a), rank2)
            finish(h, s, t1, t2, top, n1, rank2)
            return bad

        def head_exact(h, s):
            work_ref[0] = s[0]
            work_ref[1] = s[1]
            (t1, first1), (t2, first2) = _extract_rounds((work_ref.at[0], work_ref.at[1]), PEER_N_KEYS, tb)
            fill_candidates(t1, t2)
            ((top, cfirst),) = _extract_rounds((cand_ref,), CAND_ROWS, tb)
            counts = [jnp.sum(jnp.where(cfirst >= float(CAND_STARTS[a]),
                                        jnp.where(cfirst < float(CAND_STARTS[a] + CAND_COUNTS[a]), 1.0, 0.0), 0.0),
                              axis=0, keepdims=True) for a in range(PEER_TOPK)]
            n1 = _scatter_rows(first1, counts, PEER_N_KEYS, tb, 0.0)
            rank2 = _scatter_rows(first2, [float(r) for r in range(PEER_TOPK)], PEER_N_KEYS, tb, float(PEER_TOPK))
            finish(h, s, t1, t2, top, n1, rank2)

        def per_head(h, carry):
            s = []
            for c in range(2):
                r0 = pl.multiple_of((2 * h + c) * PEER_HALF, PEER_HALF)
                q = qt_ref[pl.ds(r0, PEER_HALF), :].astype(BF16)
                s.append(jnp.dot(keys_ref[2 * h + c], q, preferred_element_type=F32))
            tie = head_by_value(h, s)

            @pl.when(tie > 0)
            def _():
                head_exact(h, s)

            return carry

        lax.fori_loop(0, PEER_HEADS, per_head, 0)

    i1_0 = pl.multiple_of(jnp.maximum(j - 1, 0) * I1_PER_TILE, I1_PER_TILE)

    def step(a_next_ref, a_prev_ref):
        a_next_ref[...] = jnp.dot(u_ref[...], htb_ref[...], preferred_element_type=F32)
        n1_rows = [n1_ref[h, pl.ds(i1_0, I1_PER_TILE), :] for h in range(PEER_HEADS)]
        e1_rows = [e1_ref[h, pl.ds(i1_0, I1_PER_TILE), :] for h in range(PEER_HEADS)]
        for c in range(PEER_TE // W_CHUNK):
            w = []
            for i in range(c * W_CHUNK // PEER_N_KEYS, (c + 1) * W_CHUNK // PEER_N_KEYS):
                a = a_prev_ref[i * PEER_N_KEYS:(i + 1) * PEER_N_KEYS, :]
                act = 0.5 * a * (1.0 + lax.erf(a * (0.5 ** 0.5)))
                gate = jnp.zeros((PEER_N_KEYS, tb), BF16)
                for h in range(PEER_HEADS):
                    n1b = jnp.broadcast_to(n1_rows[h][i:i + 1, :], (PEER_N_KEYS, tb)).astype(BF16)
                    e1b = jnp.broadcast_to(e1_rows[h][i:i + 1, :], (PEER_N_KEYS, tb)).astype(BF16)
                    gate = gate + jnp.where(rank2_ref[h] < n1b, e1b * e2_ref[h], jnp.zeros_like(gate))
                w.append(gate * act.astype(BF16))
            outt_ref[...] += jnp.dot(vt_ref[:, c * W_CHUNK:(c + 1) * W_CHUNK], jnp.concatenate(w, axis=0),
                                     preferred_element_type=F32)

    @pl.when(j % 2 == 0)
    def _even():
        step(a0_ref, a1_ref)

    @pl.when(j % 2 == 1)
    def _odd():
        step(a1_ref, a0_ref)

    @pl.when(j == pl.num_programs(1) - 1)
    def _epilogue():
        h = h_ref[...]
        y_o[...] = _layer_norm(alpha * h + outt_ref[...].T, g_ref[...], b_ref[...])


def _peer(h, w, tb, alpha):
    n = h.shape[0]
    n_tiles = w["u"].shape[0] // PEER_TE
    weights_pre = (w["wq_t"], w["keys"])
    return pl.pallas_call(
        functools.partial(_peer_kernel, tb=tb, alpha=alpha),
        grid=(n // tb, n_tiles + 1),
        in_specs=[pl.BlockSpec((tb, D_MODEL), lambda i, j: (i, 0))]
        + [_full(a.shape) for a in weights_pre]
        + [pl.BlockSpec((PEER_TE, D_MODEL), lambda i, j: (jnp.minimum(j, n_tiles - 1), 0)),
           pl.BlockSpec((D_MODEL, PEER_TE), lambda i, j: (0, jnp.maximum(j - 1, 0))),
           _full(w["ln2_g"].shape), _full(w["ln2_b"].shape)],
        out_specs=pl.BlockSpec((tb, D_MODEL), lambda i, j: (i, 0)),
        out_shape=jax.ShapeDtypeStruct((n, D_MODEL), F32),
        scratch_shapes=[
            pltpu.VMEM((D_MODEL, tb), BF16),
            pltpu.VMEM((2 * PEER_HEADS * PEER_HALF, tb), F32),
            pltpu.VMEM((PEER_HEADS, PEER_N_KEYS, tb), F32),
            pltpu.VMEM((PEER_HEADS, PEER_N_KEYS, tb), F32),
            pltpu.VMEM((PEER_HEADS, PEER_N_KEYS, tb), BF16),
            pltpu.VMEM((PEER_HEADS, PEER_N_KEYS, tb), BF16),
            pltpu.VMEM((2, PEER_N_KEYS, tb), F32),
            pltpu.VMEM((CAND_ROWS, tb), F32),
            pltpu.VMEM((PEER_TE, tb), F32),
            pltpu.VMEM((PEER_TE, tb), F32),
            pltpu.VMEM((D_MODEL, tb), F32),
        ],
        compiler_params=_cparams(("parallel", "arbitrary")),
        name="peer",
    )(h, *weights_pre, w["u"], w["v_t"], w["ln2_g"], w["ln2_b"])


def _rope_tables(pos):
    inv = ROPE_THETA ** (-jnp.arange(ROPE_HALF, dtype=F32) / ROPE_HALF)
    ang = pos.astype(F32)[:, None] * inv[None, :]
    cos, sin = jnp.cos(ang), jnp.sin(ang)
    n = pos.shape[0]
    z16 = jnp.zeros((n, ROPE_HALF), F32)
    head = lambda fill: jnp.full((n, MLA_NOPE_DIM), fill, F32)
    tail = jnp.zeros((n, LANES - MLA_NOPE_DIM - MLA_ROPE_DIM), F32)
    cq = jnp.concatenate([head(1.0), cos, cos, tail], axis=1)
    ck = jnp.concatenate([head(0.0), cos, cos, tail], axis=1)
    s1 = jnp.concatenate([head(0.0), -sin, z16, tail], axis=1)
    s2 = jnp.concatenate([head(0.0), z16, sin, tail], axis=1)
    return cq, ck, s1, s2


def _pad_heads(wm, width):
    k = wm.shape[0]
    wm = wm.reshape(k, MLA_HEADS, width)
    return jnp.pad(wm, ((0, 0), (0, 0), (0, HEAD_PAD - width))).reshape(k, MLA_HEADS * HEAD_PAD)


def _prep_weights(w_in, q_norm_g, w_uq, kv_norm_g, w_uk, w_uv, w_oa, w_ob, w_out, ln1_g, ln1_b,
                  peer_wq, peer_sub_keys, peer_u, peer_v, ln2_g, ln2_b):
    o = 3 * SB_WIDTH + MLA_Q_LORA + MLA_KV_LORA
    w_kr = jnp.pad(w_in[:, o:o + MLA_ROPE_DIM], ((0, 0), (MLA_NOPE_DIM, LANES - MLA_NOPE_DIM - MLA_ROPE_DIM)))
    place = np.zeros((MLA_ROPE_DIM, LANES), np.float32)
    place[np.arange(MLA_ROPE_DIM), MLA_NOPE_DIM + np.arange(MLA_ROPE_DIM)] = 1.0
    og = o + MLA_ROPE_DIM
    row = lambda a: a.reshape(1, -1).astype(F32)
    return {
        "w_all": jnp.concatenate([w_in[:, :o], w_kr], axis=1).astype(BF16),
        "q_g": row(q_norm_g), "kv_g": row(kv_norm_g),
        "w_uq": _pad_heads(w_uq, MLA_NOPE_DIM + MLA_ROPE_DIM).astype(BF16),
        "w_uk": _pad_heads(w_uk, MLA_NOPE_DIM).astype(BF16),
        "w_uv": w_uv.astype(BF16),
        "place": jnp.asarray(place, dtype=BF16),
        "w_ga": w_in[:, og:og + D_MODEL].astype(BF16), "w_gb": w_in[:, og + D_MODEL:og + 2 * D_MODEL].astype(BF16),
        "w_oa": w_oa.astype(BF16), "w_ob": w_ob.astype(BF16), "w_out": w_out.astype(BF16),
        "ln1_g": row(ln1_g), "ln1_b": row(ln1_b), "ln2_g": row(ln2_g), "ln2_b": row(ln2_b),
        "wq_t": peer_wq.T.astype(BF16),
        "keys": peer_sub_keys.reshape(2 * PEER_HEADS, PEER_N_KEYS, PEER_HALF).astype(BF16),
        "u": peer_u.astype(BF16),
        "v_t": peer_v.T.astype(BF16),
    }


def _tile(n, pref):
    return pref if n % pref == 0 else n


def _layer(x, pos, past, w, alpha):
    b, t, _ = x.shape
    n = b * t
    tm = _tile(t, 512)
    xf = x.reshape(n, D_MODEL)
    sbq, sbk, sbv, sbkb, sbvb, ckv, kpe, qcat, kcat, vm = _proj(xf, _rope_tables(pos), w, tm)
    r3 = lambda a: a.reshape(b, t, a.shape[-1])
    q_sb, k_sb, v_sb, q_m, k_m, v_m = map(r3, (sbq, sbkb, sbvb, qcat, kcat, vm))
    if past is None:
        q_off, tk_valid = 0, t
    else:
        p_len = past[0].shape[1]
        flat = lambda a: a.reshape((b * p_len,) + a.shape[2:])
        pkc, pvm = _cache_prep(flat(past[2]), flat(past[3]), w, _tile(b * p_len, 1024))
        p3 = lambda a: a.reshape(b, p_len, a.shape[-1])
        k_sb = jnp.concatenate([past[0].astype(BF16).reshape(b, p_len, SB_WIDTH), k_sb], axis=1)
        v_sb = jnp.concatenate([past[1].astype(BF16).reshape(b, p_len, SB_WIDTH), v_sb], axis=1)
        k_m = jnp.concatenate([p3(pkc), k_m], axis=1)
        v_m = jnp.concatenate([p3(pvm), v_m], axis=1)
        q_off, tk_valid = p_len, p_len + t
    tq = _tile(t, 512)
    tk = 256
    pad = (-k_sb.shape[1]) % tk
    if pad:
        padk = lambda a: jnp.pad(a, ((0, 0), (0, pad), (0, 0)))
        k_sb, v_sb, k_m, v_m = map(padk, (k_sb, v_sb, k_m, v_m))
    o_a = _sb_attention(q_sb, k_sb, v_sb, q_off, tq, tk)
    o_b = _mla_attention(q_m, k_m, v_m, q_off, tk_valid, tq, tk)
    h = _merge(xf, o_a.reshape(n, SB_WIDTH), o_b.reshape(n, MLA_WIDTH), w, tm, alpha)
    y = _peer(h, w, _tile(n, 512), alpha)
    state = (sbk.reshape(b, t, SB_HEADS, SB_HEAD_DIM), sbv.reshape(b, t, SB_HEADS, SB_HEAD_DIM),
             ckv.reshape(b, t, MLA_KV_LORA), kpe.reshape(b, t, MLA_ROPE_DIM))
    return y.reshape(b, t, D_MODEL), state


def kernel(x_prompt, x_sample, cache_sb_k, cache_sb_v, cache_mla_ckv, cache_mla_kpe, w_in, q_norm_g, w_uq,
           kv_norm_g, w_uk, w_uv, w_oa, w_ob, w_out, ln1_g, ln1_b, peer_wq, peer_sub_keys, peer_u, peer_v,
           ln2_g, ln2_b):
    depth = w_in.shape[0]
    alpha = (2.0 * depth) ** 0.25
    past_len = cache_sb_k.shape[2]
    pos_p = jnp.arange(x_prompt.shape[1], dtype=jnp.int32)
    pos_s = past_len + jnp.arange(x_sample.shape[1], dtype=jnp.int32)
    y_p, y_s = x_prompt, x_sample
    st_p, st_s = [], []
    for l in range(depth):
        w = _prep_weights(w_in[l], q_norm_g[l], w_uq[l], kv_norm_g[l], w_uk[l], w_uv[l], w_oa[l], w_ob[l],
                          w_out[l], ln1_g[l], ln1_b[l], peer_wq[l], peer_sub_keys[l], peer_u[l], peer_v[l],
                          ln2_g[l], ln2_b[l])
        y_p, sp = _layer(y_p, pos_p, None, w, alpha)
        y_s, ss = _layer(y_s, pos_s, (cache_sb_k[l], cache_sb_v[l], cache_mla_ckv[l], cache_mla_kpe[l]), w, alpha)
        st_p.append(sp)
        st_s.append(ss)
    stack = lambda sts, i: jnp.stack([s[i] for s in sts])
    return (y_p, y_s, stack(st_p, 0), stack(st_p, 1), stack(st_p, 2), stack(st_p, 3),
            stack(st_s, 0), stack(st_s, 1), stack(st_s, 2), stack(st_s, 3))
```

```python
import functools

import numpy as np
import jax
import jax.numpy as jnp
from jax import lax
from jax.experimental import pallas as pl
from jax.experimental.pallas import tpu as pltpu

F32 = jnp.float32
BF16 = jnp.bfloat16

D_MODEL = 1024
CHUNK = 64
SB_HEADS = 8
SB_HEAD_DIM = 64
SB_WIDTH = SB_HEADS * SB_HEAD_DIM
MLA_HEADS = 8
MLA_NOPE_DIM = 64
MLA_ROPE_DIM = 32
MLA_V_DIM = 64
MLA_Q_LORA = 384
MLA_KV_LORA = 256
MLA_WIDTH = MLA_HEADS * MLA_V_DIM
ROPE_THETA = 10000.0
PEER_HEADS = 8
PEER_N_KEYS = 128
PEER_TOPK = 16
PEER_HALF = 128
EPS = 1e-5
NEG_INF = -1e30
LOG2_E = 1.4426950408889634
EXP_ZERO_F32 = -104.0

LANES = 128
HEAD_PAD = LANES
ROPE_HALF = MLA_ROPE_DIM // 2
VMEM_LIMIT = 56 * 2**20


def _cparams(sem):
    return pltpu.CompilerParams(dimension_semantics=sem, vmem_limit_bytes=VMEM_LIMIT)


def _full(shape):
    nd = len(shape)
    return pl.BlockSpec(shape, lambda *_: (0,) * nd)


def _rms(x, g):
    ms = jnp.mean(x * x, axis=-1, keepdims=True)
    return x * lax.rsqrt(ms + EPS) * g


def _layer_norm(x, g, b):
    mu = jnp.mean(x, axis=-1, keepdims=True)
    xc = x - mu
    var = jnp.mean(xc * xc, axis=-1, keepdims=True)
    return xc * lax.rsqrt(var + EPS) * g + b


def _rope_lanes(x, c, s1, s2):
    return x * c + pltpu.roll(x, LANES - ROPE_HALF, 1) * s1 + pltpu.roll(x, ROPE_HALF, 1) * s2


W_ALL = 3 * SB_WIDTH + MLA_Q_LORA + MLA_KV_LORA + LANES


def _proj_kernel(x_ref, cq_ref, ck_ref, s1_ref, s2_ref, wall_ref, qg_ref, kvg_ref, wuq_ref, wuk_ref, wuv_ref,
                 sbq_o, sbk_o, sbv_o, sbkb_o, sbvb_o, ckv_o, kpe_o, qcat_o, kcat_o, vm_o):
    xb = x_ref[...].astype(BF16)
    proj = jnp.dot(xb, wall_ref[...], preferred_element_type=F32)
    sbq_o[...] = (proj[:, 0:SB_WIDTH] * (SB_HEAD_DIM ** -0.5)).astype(BF16)
    k = proj[:, SB_WIDTH:2 * SB_WIDTH]
    v = proj[:, 2 * SB_WIDTH:3 * SB_WIDTH]
    for h in range(SB_HEADS):
        sl = slice(h * SB_HEAD_DIM, (h + 1) * SB_HEAD_DIM)
        sbk_o[:, h, :] = k[:, sl]
        sbv_o[:, h, :] = v[:, sl]
    sbkb_o[...] = k.astype(BF16)
    sbvb_o[...] = v.astype(BF16)
    o = 3 * SB_WIDTH
    cqn = _rms(proj[:, o:o + MLA_Q_LORA], qg_ref[...])
    q = jnp.dot(cqn.astype(BF16), wuq_ref[...], preferred_element_type=F32)
    o += MLA_Q_LORA
    ckv = _rms(proj[:, o:o + MLA_KV_LORA], kvg_ref[...])
    ckv_o[...] = ckv
    o += MLA_KV_LORA
    cb = ckv.astype(BF16)
    kn = jnp.dot(cb, wuk_ref[...], preferred_element_type=F32)
    vm_o[...] = jnp.dot(cb, wuv_ref[...], preferred_element_type=F32).astype(BF16)
    s1 = s1_ref[...]
    s2 = s2_ref[...]
    kp = _rope_lanes(proj[:, o:o + LANES], ck_ref[...], s1, s2)
    kpe_o[...] = kp[:, MLA_NOPE_DIM:MLA_NOPE_DIM + MLA_ROPE_DIM]
    cq = cq_ref[...]
    for h in range(MLA_HEADS):
        sl = slice(h * HEAD_PAD, (h + 1) * HEAD_PAD)
        qcat_o[:, sl] = _rope_lanes(q[:, sl], cq, s1, s2).astype(BF16)
        kcat_o[:, sl] = (kn[:, sl] + kp).astype(BF16)


def _proj(x, tabs, w, tm):
    n = x.shape[0]
    rt = tabs[0].shape[0] // tm
    tok = lambda width: pl.BlockSpec((tm, width), lambda i: (i, 0))
    tab = pl.BlockSpec((tm, LANES), lambda i: (i % rt, 0))
    outs = [(SB_WIDTH, BF16), (SB_WIDTH, F32), (SB_WIDTH, F32), (SB_WIDTH, BF16), (SB_WIDTH, BF16),
            (MLA_KV_LORA, F32), (MLA_ROPE_DIM, F32), (MLA_HEADS * HEAD_PAD, BF16), (MLA_HEADS * HEAD_PAD, BF16),
            (MLA_WIDTH, BF16)]
    weights = (w["w_all"], w["q_g"], w["kv_g"], w["w_uq"], w["w_uk"], w["w_uv"])
    out_specs = [tok(wd) for wd, _ in outs]
    out_shape = [jax.ShapeDtypeStruct((n, wd), dt) for wd, dt in outs]
    for i in (1, 2):
        out_specs[i] = pl.BlockSpec((tm, SB_HEADS, SB_HEAD_DIM), lambda i: (i, 0, 0))
        out_shape[i] = jax.ShapeDtypeStruct((n, SB_HEADS, SB_HEAD_DIM), F32)
    return pl.pallas_call(
        _proj_kernel,
        grid=(n // tm,),
        in_specs=[tok(D_MODEL), tab, tab, tab, tab] + [_full(a.shape) for a in weights],
        out_specs=out_specs,
        out_shape=out_shape,
        compiler_params=_cparams(("parallel",)),
        name="proj",
    )(x, *tabs, *weights)


def _cache_kernel(ckv_ref, kpe_ref, wuk_ref, wuv_ref, place_ref, kcat_o, vm_o):
    cb = ckv_ref[...].astype(BF16)
    kn = jnp.dot(cb, wuk_ref[...], preferred_element_type=F32)
    vm_o[...] = jnp.dot(cb, wuv_ref[...], preferred_element_type=F32).astype(BF16)
    kp = jnp.dot(kpe_ref[...].astype(BF16), place_ref[...], preferred_element_type=F32)
    for h in range(MLA_HEADS):
        sl = slice(h * HEAD_PAD, (h + 1) * HEAD_PAD)
        kcat_o[:, sl] = (kn[:, sl] + kp).astype(BF16)


def _cache_prep(ckv, kpe, w, tm):
    n = ckv.shape[0]
    tok = lambda width: pl.BlockSpec((tm, width), lambda i: (i, 0))
    outs = [(MLA_HEADS * HEAD_PAD, BF16), (MLA_WIDTH, BF16)]
    weights = (w["w_uk"], w["w_uv"], w["place"])
    return pl.pallas_call(
        _cache_kernel,
        grid=(n // tm,),
        in_specs=[tok(MLA_KV_LORA), tok(MLA_ROPE_DIM)] + [_full(a.shape) for a in weights],
        out_specs=[tok(wd) for wd, _ in outs],
        out_shape=[jax.ShapeDtypeStruct((n, wd), dt) for wd, dt in outs],
        compiler_params=_cparams(("parallel",)),
        name="cache_prep",
    )(ckv, kpe, *weights)


def _loop(n, step):
    def one(i, carry):
        step(i)
        return carry

    lax.fori_loop(0, n, one, 0)


def _sb_kernel(q_ref, k_ref, v_ref, tri_ref, o_ref, acc_ref, c_ref, *, tq, tk, nk, q_off):
    qi = pl.program_id(2)
    q = q_ref[0]
    lane = lax.broadcasted_iota(jnp.int32, (tq, LANES), 1)
    zero = jnp.zeros_like(q)
    qs = (jnp.where(lane < SB_HEAD_DIM, q, zero), jnp.where(lane >= SB_HEAD_DIM, q, zero))
    q_lo = q_off + qi * tq
    nkb = jnp.minimum(nk, (q_lo + tq - 1 + tk - 1) // tk)
    kb_diag = jnp.minimum(q_lo // tk, nkb)
    acc_ref[...] = jnp.zeros_like(acc_ref)
    c_ref[...] = jnp.zeros_like(c_ref)
    q_pos = lax.broadcasted_iota(jnp.int32, (tk, tq), 1) + q_lo
    krow = lax.broadcasted_iota(jnp.int32, (tk, tq), 0)
    tri = tri_ref[...]

    def blocks(kbs, masked):
        k0s = [pl.multiple_of(kb * tk, tk) for kb in kbs]
        zs = [[lax.dot_general(k_ref[0, pl.ds(k0, tk), :], qs[h], (((1,), (1,)), ((), ())),
                               preferred_element_type=F32) for h in range(2)] for k0 in k0s]
        stage = []
        for k0, z2 in zip(k0s, zs):
            causal = (krow + k0) < q_pos if masked else None
            for z in z2:
                lb = jnp.minimum(z, 0.0) - jnp.log(1.0 + jnp.exp(-jnp.abs(z)))
                lom = lb - z
                if masked:
                    lom = jnp.where(causal, lom, 0.0)
                hi = lom.astype(BF16)
                lo = (lom - hi.astype(F32)).astype(BF16)
                suf = jnp.dot(tri, jnp.concatenate([hi, lo], axis=0), preferred_element_type=F32)
                stage.append((lb, suf, suf[0:1, :] + lom[0:1, :], causal))
        for b, k0 in enumerate(k0s):
            v = v_ref[0, pl.ds(k0, tk), :]
            for h in range(2):
                lb, suf, tot, causal = stage[2 * b + h]
                c = c_ref[h, 0:1, :]
                a = jnp.exp(lb + suf + c)
                if masked:
                    a = jnp.where(causal, a, 0.0)
                acc_ref[h] += lax.dot_general(v, a.astype(BF16), (((0,), (0,)), ((), ())),
                                              preferred_element_type=F32)
                c_ref[h] = jnp.broadcast_to(c + tot, (8, tq))

    n_diag = nkb - kb_diag
    _loop(n_diag // 2, lambda i: blocks((nkb - 1 - 2 * i, nkb - 2 - 2 * i), True))
    _loop(n_diag % 2, lambda _: blocks((kb_diag,), True))

    def live():
        return (jnp.max(c_ref[...]) > EXP_ZERO_F32).astype(jnp.int32)

    def more(state):
        return jnp.logical_and(state[0] < kb_diag, state[1] > 0)

    def back(state):
        blocks((kb_diag - 1 - state[0],), False)
        return state[0] + 1, live()

    lax.while_loop(more, back, (jnp.int32(0), live()))
    sub = lax.broadcasted_iota(jnp.int32, (LANES, tq), 0)
    o_ref[0] = jnp.where(sub < SB_HEAD_DIM, acc_ref[0], acc_ref[1]).T.astype(BF16)


def _suffix_matrix(tk):
    s = np.arange(tk)[:, None]
    j = np.arange(tk)[None, :]
    tri = (j > s).astype(np.float32)
    return jnp.asarray(np.concatenate([tri, tri], axis=1), dtype=BF16)


def _sb_attention(q, k, v, q_off, tq, tk):
    b, t_q, _ = q.shape
    t_k = k.shape[1]
    nk = t_k // tk
    kern = functools.partial(_sb_kernel, tq=tq, tk=tk, nk=nk, q_off=q_off)
    kv_spec = pl.BlockSpec((1, t_k, LANES), lambda bi, p, qi: (bi, 0, p))
    return pl.pallas_call(
        kern,
        grid=(b, SB_WIDTH // LANES, t_q // tq),
        in_specs=[pl.BlockSpec((1, tq, LANES), lambda bi, p, qi: (bi, qi, p)), kv_spec, kv_spec,
                  _full((tk, 2 * tk))],
        out_specs=pl.BlockSpec((1, tq, LANES), lambda bi, p, qi: (bi, qi, p)),
        out_shape=jax.ShapeDtypeStruct((b, t_q, SB_WIDTH), BF16),
        scratch_shapes=[pltpu.VMEM((2, LANES, tq), F32), pltpu.VMEM((2, 8, tq), F32)],
        compiler_params=_cparams(("parallel", "parallel", "arbitrary")),
        name="sb_attention",
    )(q, k, v, _suffix_matrix(tk))


def _mla_kernel(qa_ref, qb_ref, ka_ref, kb_ref, v_ref, o_ref, acc_ref, m_ref, l_ref, *, tq, tk, nk, q_off, tk_valid):
    qi = pl.program_id(2)
    q_lo = q_off + qi * tq
    k_hi = ((q_lo + tq - 1) // CHUNK + 1) * CHUNK
    nkb = jnp.minimum(nk, (k_hi + tk - 1) // tk)
    n_free = jnp.minimum(jnp.minimum(((q_lo // CHUNK + 1) * CHUNK) // tk, tk_valid // tk), nkb)
    acc_ref[...] = jnp.zeros_like(acc_ref)
    l_ref[...] = jnp.zeros_like(l_ref)
    m_ref[...] = jnp.full_like(m_ref, NEG_INF)
    q_chunk = (lax.broadcasted_iota(jnp.int32, (tk, tq), 1) + q_lo) // CHUNK
    krow = lax.broadcasted_iota(jnp.int32, (tk, tq), 0)
    c2 = (MLA_NOPE_DIM + MLA_ROPE_DIM) ** -0.5 * LOG2_E
    qs = (qa_ref[0], qb_ref[0])
    k_refs = (ka_ref, kb_ref)

    def blocks(kbs, masked):
        k0s = [pl.multiple_of(kb * tk, tk) for kb in kbs]
        raws = [[lax.dot_general(k_refs[h][0, pl.ds(k0, tk), :], qs[h], (((1,), (1,)), ((), ())),
                                 preferred_element_type=F32) for h in range(2)] for k0 in k0s]
        for k0, raw2 in zip(k0s, raws):
            v = v_ref[0, pl.ds(k0, tk), :]
            if masked:
                kidx = krow + k0
                mask = jnp.where(kidx < tk_valid, kidx // CHUNK, q_chunk + 1) <= q_chunk
            for h in range(2):
                raw = jnp.where(mask, raw2[h], NEG_INF) if masked else raw2[h]
                m_prev = m_ref[h, 0:1, :]
                m_next = jnp.maximum(m_prev, jnp.max(raw, axis=0, keepdims=True))
                p = jnp.exp2((raw - m_next) * c2)
                alpha = jnp.exp2((m_prev - m_next) * c2)
                l_next = alpha * l_ref[h, 0:1, :] + jnp.sum(p, axis=0, keepdims=True)
                pv = lax.dot_general(v, p.astype(BF16), (((0,), (0,)), ((), ())), preferred_element_type=F32)
                acc_ref[h] = alpha * acc_ref[h] + pv
                l_ref[h] = jnp.broadcast_to(l_next, (8, tq))
                m_ref[h] = jnp.broadcast_to(m_next, (8, tq))

    n4 = n_free // 4
    _loop(n4, lambda i: blocks(tuple(4 * i + d for d in range(4)), False))
    _loop((n_free % 4) // 2, lambda _: blocks((4 * n4, 4 * n4 + 1), False))
    _loop(n_free % 2, lambda _: blocks((n_free - 1,), False))
    n_edge = nkb - n_free
    _loop(n_edge // 2, lambda i: blocks((n_free + 2 * i, n_free + 2 * i + 1), True))
    _loop(n_edge % 2, lambda _: blocks((nkb - 1,), True))
    sub = lax.broadcasted_iota(jnp.int32, (LANES, tq), 0)
    out_t = jnp.where(sub < MLA_V_DIM, acc_ref[0] / l_ref[0, 0:1, :], acc_ref[1] / l_ref[1, 0:1, :])
    o_ref[0] = out_t.T.astype(BF16)


def _mla_attention(q, k, v, q_off, tk_valid, tq, tk):
    b, t_q, _ = q.shape
    t_k = k.shape[1]
    nk = t_k // tk
    kern = functools.partial(_mla_kernel, tq=tq, tk=tk, nk=nk, q_off=q_off, tk_valid=tk_valid)
    q_spec = lambda o: pl.BlockSpec((1, tq, LANES), lambda bi, p, qi: (bi, qi, 2 * p + o))
    k_spec = lambda o: pl.BlockSpec((1, t_k, LANES), lambda bi, p, qi: (bi, 0, 2 * p + o))
    return pl.pallas_call(
        kern,
        grid=(b, MLA_HEADS // 2, t_q // tq),
        in_specs=[q_spec(0), q_spec(1), k_spec(0), k_spec(1),
                  pl.BlockSpec((1, t_k, LANES), lambda bi, p, qi: (bi, 0, p))],
        out_specs=pl.BlockSpec((1, tq, LANES), lambda bi, p, qi: (bi, qi, p)),
        out_shape=jax.ShapeDtypeStruct((b, t_q, MLA_WIDTH), BF16),
        scratch_shapes=[pltpu.VMEM((2, LANES, tq), F32), pltpu.VMEM((2, 8, tq), F32), pltpu.VMEM((2, 8, tq), F32)],
        compiler_params=_cparams(("parallel", "parallel", "arbitrary")),
        name="mla_attention",
    )(q, q, k, k, v)


def _merge_kernel(x_ref, oa_ref, ob_ref, wga_ref, wgb_ref, woa_ref, wob_ref, wout_ref, g_ref, b_ref, h_o, *, alpha):
    x = x_ref[...]
    xb = x.astype(BF16)
    ga = jax.nn.sigmoid(jnp.dot(xb, wga_ref[...], preferred_element_type=F32))
    gb = jax.nn.sigmoid(jnp.dot(xb, wgb_ref[...], preferred_element_type=F32))
    merged = (ga * jnp.dot(oa_ref[...], woa_ref[...], preferred_element_type=F32)
              + gb * jnp.dot(ob_ref[...], wob_ref[...], preferred_element_type=F32))
    y = alpha * x + jnp.dot(merged.astype(BF16), wout_ref[...], preferred_element_type=F32)
    h_o[...] = _layer_norm(y, g_ref[...], b_ref[...])


def _merge(x, oa, ob, w, tm, alpha):
    n = x.shape[0]
    tok = lambda width: pl.BlockSpec((tm, width), lambda i: (i, 0))
    weights = (w["w_ga"], w["w_gb"], w["w_oa"], w["w_ob"], w["w_out"], w["ln1_g"], w["ln1_b"])
    return pl.pallas_call(
        functools.partial(_merge_kernel, alpha=alpha),
        grid=(n // tm,),
        in_specs=[tok(D_MODEL), tok(SB_WIDTH), tok(MLA_WIDTH)] + [_full(a.shape) for a in weights],
        out_specs=tok(D_MODEL),
        out_shape=jax.ShapeDtypeStruct((n, D_MODEL), F32),
        compiler_params=_cparams(("parallel",)),
        name="merge",
    )(x, oa, ob, *weights)


PEER_TE = 1024
I1_PER_TILE = PEER_TE // PEER_N_KEYS
W_CHUNK = 512
CAND_COUNTS = tuple(PEER_TOPK // (a + 1) for a in range(PEER_TOPK))
CAND_STARTS = tuple(int(x) for x in np.cumsum((0,) + CAND_COUNTS[:-1]))
N_CAND = sum(CAND_COUNTS)
CAND_ROWS = -(-N_CAND // 8) * 8


def _extract_rounds(s_refs, n_rows, tb):
    row = lax.broadcasted_iota(jnp.int32, (n_rows, tb), 0).astype(F32)
    slot = lax.broadcasted_iota(jnp.int32, (PEER_TOPK, tb), 0)

    def body(r, carry):
        out = []
        for s_ref, (vals, firsts) in zip(s_refs, carry):
            s = s_ref[...]
            m = jnp.max(s, axis=0, keepdims=True)
            first = jnp.min(jnp.where(s == m, row, float(n_rows)), axis=0, keepdims=True)
            s_ref[...] = jnp.where(row == first, -jnp.inf, s)
            out.append((jnp.where(slot == r, m, vals), jnp.where(slot == r, first, firsts)))
        return tuple(out)

    init = tuple((jnp.zeros((PEER_TOPK, tb), F32), jnp.zeros((PEER_TOPK, tb), F32)) for _ in s_refs)
    return lax.fori_loop(0, PEER_TOPK, body, init)


def _remove_maxima(s_refs, tb):
    slot = lax.broadcasted_iota(jnp.int32, (PEER_TOPK, tb), 0)

    def body(r, carry):
        out = []
        for s_ref, vals in zip(s_refs, carry):
            s = s_ref[...]
            m = jnp.max(s, axis=0, keepdims=True)
            s_ref[...] = jnp.where(s == m, -jnp.inf, s)
            out.append(jnp.where(slot == r, m, vals))
        return tuple(out)

    return lax.fori_loop(0, PEER_TOPK, body, tuple(jnp.zeros((PEER_TOPK, tb), F32) for _ in s_refs))


def _removed_mismatch(s_after, expected):
    gone = jnp.where(s_after == -jnp.inf, 1.0, 0.0)
    per_col = jnp.sum(gone.reshape(-1, gone.shape[-1]), axis=0, keepdims=True)
    return (jnp.max(jnp.abs(per_col - float(expected))) > 0.0).astype(jnp.int32)


def _scatter_rows(firsts, values, n_rows, tb, fill):
    row = lax.broadcasted_iota(jnp.int32, (n_rows, tb), 0).astype(F32)
    out = jnp.full((n_rows, tb), fill, F32)
    for r in range(PEER_TOPK):
        out = jnp.where(row == firsts[r:r + 1, :], values[r], out)
    return out


def _peer_kernel(h_ref, wqt_ref, keys_ref, u_ref, vt_ref, g_ref, b_ref, y_o,
                 htb_ref, qt_ref, n1_ref, e1_ref, rank2_ref, e2_ref, work_ref, cand_ref, a0_ref, a1_ref, outt_ref,
                 *, tb, alpha):
    j = pl.program_id(1)

    @pl.when(j == 0)
    def _prologue():
        htb = h_ref[...].T.astype(BF16)
        htb_ref[...] = htb
        qt_ref[...] = jnp.dot(wqt_ref[...], htb, preferred_element_type=F32)
        outt_ref[...] = jnp.zeros_like(outt_ref)
        a1_ref[...] = jnp.zeros_like(a1_ref)

        def fill_candidates(t1, t2):
            cand_ref[...] = jnp.full_like(cand_ref, -jnp.inf)
            for a in range(PEER_TOPK):
                cand_ref[CAND_STARTS[a]:CAND_STARTS[a] + CAND_COUNTS[a], :] = t1[a:a + 1, :] + t2[0:CAND_COUNTS[a], :]

        def finish(h, s, t1, t2, top, n1, rank2):
            n1_ref[h] = n1
            rank2_ref[h] = rank2.astype(BF16)
            z = jnp.sum(jnp.exp(top - top[0:1, :]), axis=0, keepdims=True)
            e1_ref[h] = jnp.exp(s[0] - t1[0:1, :])
            e2_ref[h] = (jnp.exp(s[1] - t2[0:1, :]) / z).astype(BF16)

        def head_by_value(h, s):
            work_ref[0] = s[0]
            work_ref[1] = s[1]
            t1, t2 = _remove_maxima((work_ref.at[0], work_ref.at[1]), tb)
            bad = _removed_mismatch(work_ref[...], 2 * PEER_TOPK)
            fill_candidates(t1, t2)
            (top,) = _remove_maxima((cand_ref,), tb)
            gone = cand_ref[...] == -jnp.inf
            bad = jnp.maximum(bad, _removed_mismatch(cand_ref[...], PEER_TOPK + CAND_ROWS - N_CAND))
            n1 = jnp.zeros((PEER_N_KEYS, tb), F32)
            rank2 = jnp.full((PEER_N_KEYS, tb), float(PEER_TOPK), F32)
            for a in range(PEER_TOPK):
                cnt = jnp.sum(jnp.where(gone[CAND_STARTS[a]:CAND_STARTS[a] + CAND_COUNTS[a], :], 1.0, 0.0),
                              axis=0, keepdims=True)
                n1 = jnp.where(s[0] == t1[a:a + 1, :], cnt, n1)
                rank2 = jnp.where(s[1] == t2[a:a + 1, :], float(a), rank2)
            finish(h, s, t1, t2, top, n1, rank2)
            return bad

        def head_exact(h, s):
            work_ref[0] = s[0]
            work_ref[1] = s[1]
            (t1, first1), (t2, first2) = _extract_rounds((work_ref.at[0], work_ref.at[1]), PEER_N_KEYS, tb)
            fill_candidates(t1, t2)
            ((top, cfirst),) = _extract_rounds((cand_ref,), CAND_ROWS, tb)
            counts = [jnp.sum(jnp.where(cfirst >= float(CAND_STARTS[a]),
                                        jnp.where(cfirst < float(CAND_STARTS[a] + CAND_COUNTS[a]), 1.0, 0.0), 0.0),
                              axis=0, keepdims=True) for a in range(PEER_TOPK)]
            n1 = _scatter_rows(first1, counts, PEER_N_KEYS, tb, 0.0)
            rank2 = _scatter_rows(first2, [float(r) for r in range(PEER_TOPK)], PEER_N_KEYS, tb, float(PEER_TOPK))
            finish(h, s, t1, t2, top, n1, rank2)

        def per_head(h, carry):
            s = []
            for c in range(2):
                r0 = pl.multiple_of((2 * h + c) * PEER_HALF, PEER_HALF)
                q = qt_ref[pl.ds(r0, PEER_HALF), :].astype(BF16)
                s.append(jnp.dot(keys_ref[2 * h + c], q, preferred_element_type=F32))
            tie = head_by_value(h, s)

            @pl.when(tie > 0)
            def _():
                head_exact(h, s)

            return carry

        lax.fori_loop(0, PEER_HEADS, per_head, 0)

    i1_0 = pl.multiple_of(jnp.maximum(j - 1, 0) * I1_PER_TILE, I1_PER_TILE)

    def step(a_next_ref, a_prev_ref):
        a_next_ref[...] = jnp.dot(u_ref[...], htb_ref[...], preferred_element_type=F32)
        n1_rows = [n1_ref[h, pl.ds(i1_0, I1_PER_TILE), :] for h in range(PEER_HEADS)]
        e1_rows = [e1_ref[h, pl.ds(i1_0, I1_PER_TILE), :] for h in range(PEER_HEADS)]
        for c in range(PEER_TE // W_CHUNK):
            w = []
            for i in range(c * W_CHUNK // PEER_N_KEYS, (c + 1) * W_CHUNK // PEER_N_KEYS):
                a = a_prev_ref[i * PEER_N_KEYS:(i + 1) * PEER_N_KEYS, :]
                act = 0.5 * a * (1.0 + lax.erf(a * (0.5 ** 0.5)))
                gate = jnp.zeros((PEER_N_KEYS, tb), BF16)
                for h in range(PEER_HEADS):
                    n1b = jnp.broadcast_to(n1_rows[h][i:i + 1, :], (PEER_N_KEYS, tb)).astype(BF16)
                    e1b = jnp.broadcast_to(e1_rows[h][i:i + 1, :], (PEER_N_KEYS, tb)).astype(BF16)
                    gate = gate + jnp.where(rank2_ref[h] < n1b, e1b * e2_ref[h], jnp.zeros_like(gate))
                w.append(gate * act.astype(BF16))
            outt_ref[...] += jnp.dot(vt_ref[0, :, c * W_CHUNK:(c + 1) * W_CHUNK], jnp.concatenate(w, axis=0),
                                     preferred_element_type=F32)

    @pl.when(j % 2 == 0)
    def _even():
        step(a0_ref, a1_ref)

    @pl.when(j % 2 == 1)
    def _odd():
        step(a1_ref, a0_ref)

    @pl.when(j == pl.num_programs(1) - 1)
    def _epilogue():
        h = h_ref[...]
        y_o[...] = _layer_norm(alpha * h + outt_ref[...].T, g_ref[...], b_ref[...])


def _peer(h, w, tb, alpha):
    n = h.shape[0]
    n_tiles = w["u"].shape[0] // PEER_TE
    weights_pre = (w["wq_t"], w["keys"])
    return pl.pallas_call(
        functools.partial(_peer_kernel, tb=tb, alpha=alpha),
        grid=(n // tb, n_tiles + 1),
        in_specs=[pl.BlockSpec((tb, D_MODEL), lambda i, j: (i, 0))]
        + [_full(a.shape) for a in weights_pre]
        + [pl.BlockSpec((PEER_TE, D_MODEL), lambda i, j: (jnp.minimum(j, n_tiles - 1), 0)),
           pl.BlockSpec((1, D_MODEL, PEER_TE), lambda i, j: (jnp.maximum(j - 1, 0), 0, 0)),
           _full(w["ln2_g"].shape), _full(w["ln2_b"].shape)],
        out_specs=pl.BlockSpec((tb, D_MODEL), lambda i, j: (i, 0)),
        out_shape=jax.ShapeDtypeStruct((n, D_MODEL), F32),
        scratch_shapes=[
            pltpu.VMEM((D_MODEL, tb), BF16),
            pltpu.VMEM((2 * PEER_HEADS * PEER_HALF, tb), F32),
            pltpu.VMEM((PEER_HEADS, PEER_N_KEYS, tb), F32),
            pltpu.VMEM((PEER_HEADS, PEER_N_KEYS, tb), F32),
            pltpu.VMEM((PEER_HEADS, PEER_N_KEYS, tb), BF16),
            pltpu.VMEM((PEER_HEADS, PEER_N_KEYS, tb), BF16),
            pltpu.VMEM((2, PEER_N_KEYS, tb), F32),
            pltpu.VMEM((CAND_ROWS, tb), F32),
            pltpu.VMEM((PEER_TE, tb), F32),
            pltpu.VMEM((PEER_TE, tb), F32),
            pltpu.VMEM((D_MODEL, tb), F32),
        ],
        compiler_params=_cparams(("parallel", "arbitrary")),
        name="peer",
    )(h, *weights_pre, w["u"], w["v_t"], w["ln2_g"], w["ln2_b"])


def _rope_tables(pos):
    inv = ROPE_THETA ** (-jnp.arange(ROPE_HALF, dtype=F32) / ROPE_HALF)
    ang = pos.astype(F32)[:, None] * inv[None, :]
    cos, sin = jnp.cos(ang), jnp.sin(ang)
    n = pos.shape[0]
    z16 = jnp.zeros((n, ROPE_HALF), F32)
    head = lambda fill: jnp.full((n, MLA_NOPE_DIM), fill, F32)
    tail = jnp.zeros((n, LANES - MLA_NOPE_DIM - MLA_ROPE_DIM), F32)
    cq = jnp.concatenate([head(1.0), cos, cos, tail], axis=1)
    ck = jnp.concatenate([head(0.0), cos, cos, tail], axis=1)
    s1 = jnp.concatenate([head(0.0), -sin, z16, tail], axis=1)
    s2 = jnp.concatenate([head(0.0), z16, sin, tail], axis=1)
    return cq, ck, s1, s2


def _pad_heads(wm, width):
    k = wm.shape[0]
    wm = wm.reshape(k, MLA_HEADS, width)
    return jnp.pad(wm, ((0, 0), (0, 0), (0, HEAD_PAD - width))).reshape(k, MLA_HEADS * HEAD_PAD)


def _prep_weights(w_in, q_norm_g, w_uq, kv_norm_g, w_uk, w_uv, w_oa, w_ob, w_out, ln1_g, ln1_b,
                  peer_wq, peer_sub_keys, peer_u, peer_v, ln2_g, ln2_b):
    o = 3 * SB_WIDTH + MLA_Q_LORA + MLA_KV_LORA
    w_kr = jnp.pad(w_in[:, o:o + MLA_ROPE_DIM], ((0, 0), (MLA_NOPE_DIM, LANES - MLA_NOPE_DIM - MLA_ROPE_DIM)))
    place = np.zeros((MLA_ROPE_DIM, LANES), np.float32)
    place[np.arange(MLA_ROPE_DIM), MLA_NOPE_DIM + np.arange(MLA_ROPE_DIM)] = 1.0
    og = o + MLA_ROPE_DIM
    row = lambda a: a.reshape(1, -1).astype(F32)
    return {
        "w_all": jnp.concatenate([w_in[:, :o], w_kr], axis=1).astype(BF16),
        "q_g": row(q_norm_g), "kv_g": row(kv_norm_g),
        "w_uq": _pad_heads(w_uq, MLA_NOPE_DIM + MLA_ROPE_DIM).astype(BF16),
        "w_uk": _pad_heads(w_uk, MLA_NOPE_DIM).astype(BF16),
        "w_uv": w_uv.astype(BF16),
        "place": jnp.asarray(place, dtype=BF16),
        "w_ga": w_in[:, og:og + D_MODEL].astype(BF16), "w_gb": w_in[:, og + D_MODEL:og + 2 * D_MODEL].astype(BF16),
        "w_oa": w_oa.astype(BF16), "w_ob": w_ob.astype(BF16), "w_out": w_out.astype(BF16),
        "ln1_g": row(ln1_g), "ln1_b": row(ln1_b), "ln2_g": row(ln2_g), "ln2_b": row(ln2_b),
        "wq_t": peer_wq.T.astype(BF16),
        "keys": peer_sub_keys.reshape(2 * PEER_HEADS, PEER_N_KEYS, PEER_HALF).astype(BF16),
        "u": peer_u.astype(BF16),
        "v_t": peer_v.reshape(-1, PEER_TE, D_MODEL).transpose(0, 2, 1).astype(BF16),
    }


def _tile(n, pref):
    return pref if n % pref == 0 else n


def _layer(x, pos, past, w, alpha):
    b, t, _ = x.shape
    n = b * t
    tm = _tile(t, 512)
    xf = x.reshape(n, D_MODEL)
    sbq, sbk, sbv, sbkb, sbvb, ckv, kpe, qcat, kcat, vm = _proj(xf, _rope_tables(pos), w, tm)
    r3 = lambda a: a.reshape(b, t, a.shape[-1])
    q_sb, k_sb, v_sb, q_m, k_m, v_m = map(r3, (sbq, sbkb, sbvb, qcat, kcat, vm))
    if past is None:
        q_off, tk_valid = 0, t
    else:
        p_len = past[0].shape[1]
        flat = lambda a: a.reshape((b * p_len,) + a.shape[2:])
        pkc, pvm = _cache_prep(flat(past[2]), flat(past[3]), w, _tile(b * p_len, 1024))
        p3 = lambda a: a.reshape(b, p_len, a.shape[-1])
        k_sb = jnp.concatenate([past[0].astype(BF16).reshape(b, p_len, SB_WIDTH), k_sb], axis=1)
        v_sb = jnp.concatenate([past[1].astype(BF16).reshape(b, p_len, SB_WIDTH), v_sb], axis=1)
        k_m = jnp.concatenate([p3(pkc), k_m], axis=1)
        v_m = jnp.concatenate([p3(pvm), v_m], axis=1)
        q_off, tk_valid = p_len, p_len + t
    tq = _tile(t, 512)
    tk = 256
    pad = (-k_sb.shape[1]) % tk
    if pad:
        padk = lambda a: jnp.pad(a, ((0, 0), (0, pad), (0, 0)))
        k_sb, v_sb, k_m, v_m = map(padk, (k_sb, v_sb, k_m, v_m))
    o_a = _sb_attention(q_sb, k_sb, v_sb, q_off, tq, tk)
    o_b = _mla_attention(q_m, k_m, v_m, q_off, tk_valid, tq, tk)
    h = _merge(xf, o_a.reshape(n, SB_WIDTH), o_b.reshape(n, MLA_WIDTH), w, tm, alpha)
    y = _peer(h, w, _tile(n, 512), alpha)
    state = (sbk.reshape(b, t, SB_HEADS, SB_HEAD_DIM), sbv.reshape(b, t, SB_HEADS, SB_HEAD_DIM),
             ckv.reshape(b, t, MLA_KV_LORA), kpe.reshape(b, t, MLA_ROPE_DIM))
    return y.reshape(b, t, D_MODEL), state


def kernel(x_prompt, x_sample, cache_sb_k, cache_sb_v, cache_mla_ckv, cache_mla_kpe, w_in, q_norm_g, w_uq,
           kv_norm_g, w_uk, w_uv, w_oa, w_ob, w_out, ln1_g, ln1_b, peer_wq, peer_sub_keys, peer_u, peer_v,
           ln2_g, ln2_b):
    depth = w_in.shape[0]
    alpha = (2.0 * depth) ** 0.25
    past_len = cache_sb_k.shape[2]
    pos_p = jnp.arange(x_prompt.shape[1], dtype=jnp.int32)
    pos_s = past_len + jnp.arange(x_sample.shape[1], dtype=jnp.int32)
    y_p, y_s = x_prompt, x_sample
    st_p, st_s = [], []
    for l in range(depth):
        w = _prep_weights(w_in[l], q_norm_g[l], w_uq[l], kv_norm_g[l], w_uk[l], w_uv[l], w_oa[l], w_ob[l],
                          w_out[l], ln1_g[l], ln1_b[l], peer_wq[l], peer_sub_keys[l], peer_u[l], peer_v[l],
                          ln2_g[l], ln2_b[l])
        y_p, sp = _layer(y_p, pos_p, None, w, alpha)
        y_s, ss = _layer(y_s, pos_s, (cache_sb_k[l], cache_sb_v[l], cache_mla_ckv[l], cache_mla_kpe[l]), w, alpha)
        st_p.append(sp)
        st_s.append(ss)
    stack = lambda sts, i: jnp.stack([s[i] for s in sts])
    return (y_p, y_s, stack(st_p, 0), stack(st_p, 1), stack(st_p, 2), stack(st_p, 3),
            stack(st_s, 0), stack(st_s, 1), stack(st_s, 2), stack(st_s, 3))
```

```python
import functools

import numpy as np
import jax
import jax.numpy as jnp
from jax import lax
from jax.experimental import pallas as pl
from jax.experimental.pallas import tpu as pltpu

F32 = jnp.float32
BF16 = jnp.bfloat16

D_MODEL = 1024
CHUNK = 64
SB_HEADS = 8
SB_HEAD_DIM = 64
SB_WIDTH = SB_HEADS * SB_HEAD_DIM
MLA_HEADS = 8
MLA_NOPE_DIM = 64
MLA_ROPE_DIM = 32
MLA_V_DIM = 64
MLA_Q_LORA = 384
MLA_KV_LORA = 256
MLA_WIDTH = MLA_HEADS * MLA_V_DIM
ROPE_THETA = 10000.0
PEER_HEADS = 8
PEER_N_KEYS = 128
PEER_TOPK = 16
PEER_HALF = 128
EPS = 1e-5
NEG_INF = -1e30
LOG2_E = 1.4426950408889634
EXP_ZERO_F32 = -104.0

LANES = 128
HEAD_PAD = LANES
ROPE_HALF = MLA_ROPE_DIM // 2
VMEM_LIMIT = 56 * 2**20
TOKEN_TILE = 512
CACHE_TILE = 1024
Q_TILE = 512
K_TILE = 256


def _cparams(sem):
    return pltpu.CompilerParams(dimension_semantics=sem, vmem_limit_bytes=VMEM_LIMIT)


def _full(shape):
    nd = len(shape)
    return pl.BlockSpec(shape, lambda *_: (0,) * nd)


def _rms(x, g):
    ms = jnp.mean(x * x, axis=-1, keepdims=True)
    return x * lax.rsqrt(ms + EPS) * g


def _layer_norm(x, g, b):
    mu = jnp.mean(x, axis=-1, keepdims=True)
    xc = x - mu
    var = jnp.mean(xc * xc, axis=-1, keepdims=True)
    return xc * lax.rsqrt(var + EPS) * g + b


def _rope_lanes(x, c, s1, s2):
    return x * c + pltpu.roll(x, LANES - ROPE_HALF, 1) * s1 + pltpu.roll(x, ROPE_HALF, 1) * s2


W_ALL = 3 * SB_WIDTH + MLA_Q_LORA + MLA_KV_LORA + LANES


def _proj_kernel(x_ref, cq_ref, ck_ref, s1_ref, s2_ref, wall_ref, qg_ref, kvg_ref, wuq_ref, wuk_ref, wuv_ref,
                 sbq_o, sbk_o, sbv_o, sbkb_o, sbvb_o, ckv_o, kpe_o, qcat_o, kcat_o, vm_o):
    xb = x_ref[...].astype(BF16)
    proj = jnp.dot(xb, wall_ref[...], preferred_element_type=F32)
    sbq_o[...] = (proj[:, 0:SB_WIDTH] * (SB_HEAD_DIM ** -0.5)).astype(BF16)
    k = proj[:, SB_WIDTH:2 * SB_WIDTH]
    v = proj[:, 2 * SB_WIDTH:3 * SB_WIDTH]
    for h in range(SB_HEADS):
        sl = slice(h * SB_HEAD_DIM, (h + 1) * SB_HEAD_DIM)
        sbk_o[:, h, :] = k[:, sl]
        sbv_o[:, h, :] = v[:, sl]
    sbkb_o[...] = k.astype(BF16)
    sbvb_o[...] = v.astype(BF16)
    o = 3 * SB_WIDTH
    cqn = _rms(proj[:, o:o + MLA_Q_LORA], qg_ref[...])
    q = jnp.dot(cqn.astype(BF16), wuq_ref[...], preferred_element_type=F32)
    o += MLA_Q_LORA
    ckv = _rms(proj[:, o:o + MLA_KV_LORA], kvg_ref[...])
    ckv_o[...] = ckv
    o += MLA_KV_LORA
    cb = ckv.astype(BF16)
    kn = jnp.dot(cb, wuk_ref[...], preferred_element_type=F32)
    vm_o[...] = jnp.dot(cb, wuv_ref[...], preferred_element_type=F32).astype(BF16)
    s1 = s1_ref[...]
    s2 = s2_ref[...]
    kp = _rope_lanes(proj[:, o:o + LANES], ck_ref[...], s1, s2)
    kpe_o[...] = kp[:, MLA_NOPE_DIM:MLA_NOPE_DIM + MLA_ROPE_DIM]
    cq = cq_ref[...]
    for h in range(MLA_HEADS):
        sl = slice(h * HEAD_PAD, (h + 1) * HEAD_PAD)
        qcat_o[:, sl] = _rope_lanes(q[:, sl], cq, s1, s2).astype(BF16)
        kcat_o[:, sl] = (kn[:, sl] + kp).astype(BF16)


def _proj(x, tabs, w, tm):
    n = x.shape[0]
    rt = tabs[0].shape[0] // tm
    tok = lambda width: pl.BlockSpec((tm, width), lambda i: (i, 0))
    tab = pl.BlockSpec((tm, LANES), lambda i: (i % rt, 0))
    outs = [(SB_WIDTH, BF16), (SB_WIDTH, F32), (SB_WIDTH, F32), (SB_WIDTH, BF16), (SB_WIDTH, BF16),
            (MLA_KV_LORA, F32), (MLA_ROPE_DIM, F32), (MLA_HEADS * HEAD_PAD, BF16), (MLA_HEADS * HEAD_PAD, BF16),
            (MLA_WIDTH, BF16)]
    weights = (w["w_all"], w["q_g"], w["kv_g"], w["w_uq"], w["w_uk"], w["w_uv"])
    out_specs = [tok(wd) for wd, _ in outs]
    out_shape = [jax.ShapeDtypeStruct((n, wd), dt) for wd, dt in outs]
    for i in (1, 2):
        out_specs[i] = pl.BlockSpec((tm, SB_HEADS, SB_HEAD_DIM), lambda i: (i, 0, 0))
        out_shape[i] = jax.ShapeDtypeStruct((n, SB_HEADS, SB_HEAD_DIM), F32)
    return pl.pallas_call(
        _proj_kernel,
        grid=(n // tm,),
        in_specs=[tok(D_MODEL), tab, tab, tab, tab] + [_full(a.shape) for a in weights],
        out_specs=out_specs,
        out_shape=out_shape,
        compiler_params=_cparams(("parallel",)),
        name="proj",
    )(x, *tabs, *weights)


def _cache_kernel(ckv_ref, kpe_ref, wuk_ref, wuv_ref, place_ref, kcat_o, vm_o):
    cb = ckv_ref[...].astype(BF16)
    kn = jnp.dot(cb, wuk_ref[...], preferred_element_type=F32)
    vm_o[...] = jnp.dot(cb, wuv_ref[...], preferred_element_type=F32).astype(BF16)
    kp = jnp.dot(kpe_ref[...].astype(BF16), place_ref[...], preferred_element_type=F32)
    for h in range(MLA_HEADS):
        sl = slice(h * HEAD_PAD, (h + 1) * HEAD_PAD)
        kcat_o[:, sl] = (kn[:, sl] + kp).astype(BF16)


def _cache_prep(ckv, kpe, w, tm):
    n = ckv.shape[0]
    tok = lambda width: pl.BlockSpec((tm, width), lambda i: (i, 0))
    outs = [(MLA_HEADS * HEAD_PAD, BF16), (MLA_WIDTH, BF16)]
    weights = (w["w_uk"], w["w_uv"], w["place"])
    return pl.pallas_call(
        _cache_kernel,
        grid=(n // tm,),
        in_specs=[tok(MLA_KV_LORA), tok(MLA_ROPE_DIM)] + [_full(a.shape) for a in weights],
        out_specs=[tok(wd) for wd, _ in outs],
        out_shape=[jax.ShapeDtypeStruct((n, wd), dt) for wd, dt in outs],
        compiler_params=_cparams(("parallel",)),
        name="cache_prep",
    )(ckv, kpe, *weights)


def _loop(n, step):
    def one(i, carry):
        step(i)
        return carry

    lax.fori_loop(0, n, one, 0)


def _sb_kernel(q_ref, k_ref, v_ref, tri_ref, o_ref, acc_ref, c_ref, *, tq, tk, nk, q_off):
    qi = pl.program_id(2)
    q = q_ref[0]
    lane = lax.broadcasted_iota(jnp.int32, (tq, LANES), 1)
    zero = jnp.zeros_like(q)
    qs = (jnp.where(lane < SB_HEAD_DIM, q, zero), jnp.where(lane >= SB_HEAD_DIM, q, zero))
    q_lo = q_off + qi * tq
    nkb = jnp.minimum(nk, (q_lo + tq - 1 + tk - 1) // tk)
    kb_diag = jnp.minimum(q_lo // tk, nkb)
    acc_ref[...] = jnp.zeros_like(acc_ref)
    c_ref[...] = jnp.zeros_like(c_ref)
    q_pos = lax.broadcasted_iota(jnp.int32, (tk, tq), 1) + q_lo
    krow = lax.broadcasted_iota(jnp.int32, (tk, tq), 0)
    tri = tri_ref[...]

    def blocks(kbs, masked):
        k0s = [pl.multiple_of(kb * tk, tk) for kb in kbs]
        zs = [[lax.dot_general(k_ref[0, pl.ds(k0, tk), :], qs[h], (((1,), (1,)), ((), ())),
                               preferred_element_type=F32) for h in range(2)] for k0 in k0s]
        stage = []
        for k0, z2 in zip(k0s, zs):
            causal = (krow + k0) < q_pos if masked else None
            for z in z2:
                lb = jnp.minimum(z, 0.0) - jnp.log(1.0 + jnp.exp(-jnp.abs(z)))
                lom = lb - z
                if masked:
                    lom = jnp.where(causal, lom, 0.0)
                hi = lom.astype(BF16)
                lo = (lom - hi.astype(F32)).astype(BF16)
                suf = jnp.dot(tri, jnp.concatenate([hi, lo], axis=0), preferred_element_type=F32)
                stage.append((lb, suf, suf[0:1, :] + lom[0:1, :], causal))
        for b, k0 in enumerate(k0s):
            v = v_ref[0, pl.ds(k0, tk), :]
            for h in range(2):
                lb, suf, tot, causal = stage[2 * b + h]
                c = c_ref[h, 0:1, :]
                a = jnp.exp(lb + suf + c)
                if masked:
                    a = jnp.where(causal, a, 0.0)
                acc_ref[h] += lax.dot_general(v, a.astype(BF16), (((0,), (0,)), ((), ())),
                                              preferred_element_type=F32)
                c_ref[h] = jnp.broadcast_to(c + tot, (8, tq))

    n_diag = nkb - kb_diag
    _loop(n_diag // 2, lambda i: blocks((nkb - 1 - 2 * i, nkb - 2 - 2 * i), True))
    _loop(n_diag % 2, lambda _: blocks((kb_diag,), True))

    def live():
        return (jnp.max(c_ref[...]) > EXP_ZERO_F32).astype(jnp.int32)

    def more(state):
        return jnp.logical_and(state[0] < kb_diag, state[1] > 0)

    def back(state):
        blocks((kb_diag - 1 - state[0],), False)
        return state[0] + 1, live()

    lax.while_loop(more, back, (jnp.int32(0), live()))
    sub = lax.broadcasted_iota(jnp.int32, (LANES, tq), 0)
    o_ref[0] = jnp.where(sub < SB_HEAD_DIM, acc_ref[0], acc_ref[1]).T.astype(BF16)


def _suffix_matrix(tk):
    s = np.arange(tk)[:, None]
    j = np.arange(tk)[None, :]
    tri = (j > s).astype(np.float32)
    return jnp.asarray(np.concatenate([tri, tri], axis=1), dtype=BF16)


def _sb_attention(q, k, v, q_off, tq, tk):
    b, t_q, _ = q.shape
    t_k = k.shape[1]
    nk = t_k // tk
    kern = functools.partial(_sb_kernel, tq=tq, tk=tk, nk=nk, q_off=q_off)
    kv_spec = pl.BlockSpec((1, t_k, LANES), lambda bi, p, qi: (bi, 0, p))
    return pl.pallas_call(
        kern,
        grid=(b, SB_WIDTH // LANES, t_q // tq),
        in_specs=[pl.BlockSpec((1, tq, LANES), lambda bi, p, qi: (bi, qi, p)), kv_spec, kv_spec,
                  _full((tk, 2 * tk))],
        out_specs=pl.BlockSpec((1, tq, LANES), lambda bi, p, qi: (bi, qi, p)),
        out_shape=jax.ShapeDtypeStruct((b, t_q, SB_WIDTH), BF16),
        scratch_shapes=[pltpu.VMEM((2, LANES, tq), F32), pltpu.VMEM((2, 8, tq), F32)],
        compiler_params=_cparams(("parallel", "parallel", "arbitrary")),
        name="sb_attention",
    )(q, k, v, _suffix_matrix(tk))


def _mla_kernel(qa_ref, qb_ref, ka_ref, kb_ref, v_ref, o_ref, acc_ref, m_ref, l_ref, *, tq, tk, nk, q_off, tk_valid):
    qi = pl.program_id(2)
    q_lo = q_off + qi * tq
    k_hi = ((q_lo + tq - 1) // CHUNK + 1) * CHUNK
    nkb = jnp.minimum(nk, (k_hi + tk - 1) // tk)
    n_free = jnp.minimum(jnp.minimum(((q_lo // CHUNK + 1) * CHUNK) // tk, tk_valid // tk), nkb)
    acc_ref[...] = jnp.zeros_like(acc_ref)
    l_ref[...] = jnp.zeros_like(l_ref)
    m_ref[...] = jnp.full_like(m_ref, NEG_INF)
    q_chunk = (lax.broadcasted_iota(jnp.int32, (tk, tq), 1) + q_lo) // CHUNK
    krow = lax.broadcasted_iota(jnp.int32, (tk, tq), 0)
    c2 = (MLA_NOPE_DIM + MLA_ROPE_DIM) ** -0.5 * LOG2_E
    qs = (qa_ref[0], qb_ref[0])
    k_refs = (ka_ref, kb_ref)

    def blocks(kbs, masked):
        k0s = [pl.multiple_of(kb * tk, tk) for kb in kbs]
        raws = [[lax.dot_general(k_refs[h][0, pl.ds(k0, tk), :], qs[h], (((1,), (1,)), ((), ())),
                                 preferred_element_type=F32) for h in range(2)] for k0 in k0s]
        for k0, raw2 in zip(k0s, raws):
            v = v_ref[0, pl.ds(k0, tk), :]
            if masked:
                kidx = krow + k0
                mask = jnp.where(kidx < tk_valid, kidx // CHUNK, q_chunk + 1) <= q_chunk
            for h in range(2):
                raw = jnp.where(mask, raw2[h], NEG_INF) if masked else raw2[h]
                m_prev = m_ref[h, 0:1, :]
                m_next = jnp.maximum(m_prev, jnp.max(raw, axis=0, keepdims=True))
                p = jnp.exp2((raw - m_next) * c2)
                alpha = jnp.exp2((m_prev - m_next) * c2)
                l_next = alpha * l_ref[h, 0:1, :] + jnp.sum(p, axis=0, keepdims=True)
                pv = lax.dot_general(v, p.astype(BF16), (((0,), (0,)), ((), ())), preferred_element_type=F32)
                acc_ref[h] = alpha * acc_ref[h] + pv
                l_ref[h] = jnp.broadcast_to(l_next, (8, tq))
                m_ref[h] = jnp.broadcast_to(m_next, (8, tq))

    n4 = n_free // 4
    _loop(n4, lambda i: blocks(tuple(4 * i + d for d in range(4)), False))
    _loop((n_free % 4) // 2, lambda _: blocks((4 * n4, 4 * n4 + 1), False))
    _loop(n_free % 2, lambda _: blocks((n_free - 1,), False))
    n_edge = nkb - n_free
    _loop(n_edge // 2, lambda i: blocks((n_free + 2 * i, n_free + 2 * i + 1), True))
    _loop(n_edge % 2, lambda _: blocks((nkb - 1,), True))
    sub = lax.broadcasted_iota(jnp.int32, (LANES, tq), 0)
    out_t = jnp.where(sub < MLA_V_DIM, acc_ref[0] / l_ref[0, 0:1, :], acc_ref[1] / l_ref[1, 0:1, :])
    o_ref[0] = out_t.T.astype(BF16)


def _mla_attention(q, k, v, q_off, tk_valid, tq, tk):
    b, t_q, _ = q.shape
    t_k = k.shape[1]
    nk = t_k // tk
    kern = functools.partial(_mla_kernel, tq=tq, tk=tk, nk=nk, q_off=q_off, tk_valid=tk_valid)
    q_spec = lambda o: pl.BlockSpec((1, tq, LANES), lambda bi, p, qi: (bi, qi, 2 * p + o))
    k_spec = lambda o: pl.BlockSpec((1, t_k, LANES), lambda bi, p, qi: (bi, 0, 2 * p + o))
    return pl.pallas_call(
        kern,
        grid=(b, MLA_HEADS // 2, t_q // tq),
        in_specs=[q_spec(0), q_spec(1), k_spec(0), k_spec(1),
                  pl.BlockSpec((1, t_k, LANES), lambda bi, p, qi: (bi, 0, p))],
        out_specs=pl.BlockSpec((1, tq, LANES), lambda bi, p, qi: (bi, qi, p)),
        out_shape=jax.ShapeDtypeStruct((b, t_q, MLA_WIDTH), BF16),
        scratch_shapes=[pltpu.VMEM((2, LANES, tq), F32), pltpu.VMEM((2, 8, tq), F32), pltpu.VMEM((2, 8, tq), F32)],
        compiler_params=_cparams(("parallel", "parallel", "arbitrary")),
        name="mla_attention",
    )(q, q, k, k, v)


def _merge_kernel(x_ref, oa_ref, ob_ref, wga_ref, wgb_ref, woa_ref, wob_ref, wout_ref, g_ref, b_ref, h_o, *, alpha):
    x = x_ref[...]
    xb = x.astype(BF16)
    ga = jax.nn.sigmoid(jnp.dot(xb, wga_ref[...], preferred_element_type=F32))
    gb = jax.nn.sigmoid(jnp.dot(xb, wgb_ref[...], preferred_element_type=F32))
    merged = (ga * jnp.dot(oa_ref[...], woa_ref[...], preferred_element_type=F32)
              + gb * jnp.dot(ob_ref[...], wob_ref[...], preferred_element_type=F32))
    y = alpha * x + jnp.dot(merged.astype(BF16), wout_ref[...], preferred_element_type=F32)
    h_o[...] = _layer_norm(y, g_ref[...], b_ref[...])


def _merge(x, oa, ob, w, tm, alpha):
    n = x.shape[0]
    tok = lambda width: pl.BlockSpec((tm, width), lambda i: (i, 0))
    weights = (w["w_ga"], w["w_gb"], w["w_oa"], w["w_ob"], w["w_out"], w["ln1_g"], w["ln1_b"])
    return pl.pallas_call(
        functools.partial(_merge_kernel, alpha=alpha),
        grid=(n // tm,),
        in_specs=[tok(D_MODEL), tok(SB_WIDTH), tok(MLA_WIDTH)] + [_full(a.shape) for a in weights],
        out_specs=tok(D_MODEL),
        out_shape=jax.ShapeDtypeStruct((n, D_MODEL), F32),
        compiler_params=_cparams(("parallel",)),
        name="merge",
    )(x, oa, ob, *weights)


PEER_TE = 1024
I1_PER_TILE = PEER_TE // PEER_N_KEYS
W_CHUNK = 512
CAND_COUNTS = tuple(PEER_TOPK // (a + 1) for a in range(PEER_TOPK))
CAND_STARTS = tuple(int(x) for x in np.cumsum((0,) + CAND_COUNTS[:-1]))
N_CAND = sum(CAND_COUNTS)
CAND_ROWS = -(-N_CAND // 8) * 8


def _extract_rounds(s_refs, n_rows, tb):
    row = lax.broadcasted_iota(jnp.int32, (n_rows, tb), 0).astype(F32)
    slot = lax.broadcasted_iota(jnp.int32, (PEER_TOPK, tb), 0)

    def body(r, carry):
        out = []
        for s_ref, (vals, firsts) in zip(s_refs, carry):
            s = s_ref[...]
            m = jnp.max(s, axis=0, keepdims=True)
            first = jnp.min(jnp.where(s == m, row, float(n_rows)), axis=0, keepdims=True)
            s_ref[...] = jnp.where(row == first, -jnp.inf, s)
            out.append((jnp.where(slot == r, m, vals), jnp.where(slot == r, first, firsts)))
        return tuple(out)

    init = tuple((jnp.zeros((PEER_TOPK, tb), F32), jnp.zeros((PEER_TOPK, tb), F32)) for _ in s_refs)
    return lax.fori_loop(0, PEER_TOPK, body, init)


def _remove_maxima(s_refs, tb):
    slot = lax.broadcasted_iota(jnp.int32, (PEER_TOPK, tb), 0)

    def body(r, carry):
        out = []
        for s_ref, vals in zip(s_refs, carry):
            s = s_ref[...]
            m = jnp.max(s, axis=0, keepdims=True)
            s_ref[...] = jnp.where(s == m, -jnp.inf, s)
            out.append(jnp.where(slot == r, m, vals))
        return tuple(out)

    return lax.fori_loop(0, PEER_TOPK, body, tuple(jnp.zeros((PEER_TOPK, tb), F32) for _ in s_refs))


def _removed_mismatch(s_after, expected):
    gone = jnp.where(s_after == -jnp.inf, 1.0, 0.0)
    per_col = jnp.sum(gone.reshape(-1, gone.shape[-1]), axis=0, keepdims=True)
    return (jnp.max(jnp.abs(per_col - float(expected))) > 0.0).astype(jnp.int32)


def _scatter_rows(firsts, values, n_rows, tb, fill):
    row = lax.broadcasted_iota(jnp.int32, (n_rows, tb), 0).astype(F32)
    out = jnp.full((n_rows, tb), fill, F32)
    for r in range(PEER_TOPK):
        out = jnp.where(row == firsts[r:r + 1, :], values[r], out)
    return out


def _peer_kernel(h_ref, wqt_ref, keys_ref, u_ref, vt_ref, g_ref, b_ref, y_o,
                 htb_ref, qt_ref, n1_ref, e1_ref, rank2_ref, e2_ref, work_ref, cand_ref, a0_ref, a1_ref, outt_ref,
                 *, tb, alpha):
    j = pl.program_id(1)

    @pl.when(j == 0)
    def _prologue():
        htb = h_ref[...].T.astype(BF16)
        htb_ref[...] = htb
        qt_ref[...] = jnp.dot(wqt_ref[...], htb, preferred_element_type=F32)
        outt_ref[...] = jnp.zeros_like(outt_ref)
        a1_ref[...] = jnp.zeros_like(a1_ref)

        def fill_candidates(t1, t2):
            cand_ref[...] = jnp.full_like(cand_ref, -jnp.inf)
            for a in range(PEER_TOPK):
                cand_ref[CAND_STARTS[a]:CAND_STARTS[a] + CAND_COUNTS[a], :] = t1[a:a + 1, :] + t2[0:CAND_COUNTS[a], :]

        def finish(h, s, t1, t2, top, n1, rank2):
            n1_ref[h] = n1
            rank2_ref[h] = rank2.astype(BF16)
            z = jnp.sum(jnp.exp(top - top[0:1, :]), axis=0, keepdims=True)
            e1_ref[h] = jnp.exp(s[0] - t1[0:1, :])
            e2_ref[h] = (jnp.exp(s[1] - t2[0:1, :]) / z).astype(BF16)

        def head_by_value(h, s):
            work_ref[0] = s[0]
            work_ref[1] = s[1]
            t1, t2 = _remove_maxima((work_ref.at[0], work_ref.at[1]), tb)
            bad = _removed_mismatch(work_ref[...], 2 * PEER_TOPK)
            fill_candidates(t1, t2)
            (top,) = _remove_maxima((cand_ref,), tb)
            gone = cand_ref[...] == -jnp.inf
            bad = jnp.maximum(bad, _removed_mismatch(cand_ref[...], PEER_TOPK + CAND_ROWS - N_CAND))
            n1 = jnp.zeros((PEER_N_KEYS, tb), F32)
            rank2 = jnp.full((PEER_N_KEYS, tb), float(PEER_TOPK), F32)
            for a in range(PEER_TOPK):
                cnt = jnp.sum(jnp.where(gone[CAND_STARTS[a]:CAND_STARTS[a] + CAND_COUNTS[a], :], 1.0, 0.0),
                              axis=0, keepdims=True)
                n1 = jnp.where(s[0] == t1[a:a + 1, :], cnt, n1)
                rank2 = jnp.where(s[1] == t2[a:a + 1, :], float(a), rank2)
            finish(h, s, t1, t2, top, n1, rank2)
            return bad

        def head_exact(h, s):
            work_ref[0] = s[0]
            work_ref[1] = s[1]
            (t1, first1), (t2, first2) = _extract_rounds((work_ref.at[0], work_ref.at[1]), PEER_N_KEYS, tb)
            fill_candidates(t1, t2)
            ((top, cfirst),) = _extract_rounds((cand_ref,), CAND_ROWS, tb)
            counts = [jnp.sum(jnp.where(cfirst >= float(CAND_STARTS[a]),
                                        jnp.where(cfirst < float(CAND_STARTS[a] + CAND_COUNTS[a]), 1.0, 0.0), 0.0),
                              axis=0, keepdims=True) for a in range(PEER_TOPK)]
            n1 = _scatter_rows(first1, counts, PEER_N_KEYS, tb, 0.0)
            rank2 = _scatter_rows(first2, [float(r) for r in range(PEER_TOPK)], PEER_N_KEYS, tb, float(PEER_TOPK))
            finish(h, s, t1, t2, top, n1, rank2)

        def per_head(h, carry):
            s = []
            for c in range(2):
                r0 = pl.multiple_of((2 * h + c) * PEER_HALF, PEER_HALF)
                q = qt_ref[pl.ds(r0, PEER_HALF), :].astype(BF16)
                s.append(jnp.dot(keys_ref[2 * h + c], q, preferred_element_type=F32))
            tie = head_by_value(h, s)

            @pl.when(tie > 0)
            def _():
                head_exact(h, s)

            return carry

        lax.fori_loop(0, PEER_HEADS, per_head, 0)

    i1_0 = pl.multiple_of(jnp.maximum(j - 1, 0) * I1_PER_TILE, I1_PER_TILE)

    def step(a_next_ref, a_prev_ref):
        a_next_ref[...] = jnp.dot(u_ref[...], htb_ref[...], preferred_element_type=F32)
        n1_rows = [n1_ref[h, pl.ds(i1_0, I1_PER_TILE), :] for h in range(PEER_HEADS)]
        e1_rows = [e1_ref[h, pl.ds(i1_0, I1_PER_TILE), :] for h in range(PEER_HEADS)]
        for c in range(PEER_TE // W_CHUNK):
            w = []
            for i in range(c * W_CHUNK // PEER_N_KEYS, (c + 1) * W_CHUNK // PEER_N_KEYS):
                a = a_prev_ref[i * PEER_N_KEYS:(i + 1) * PEER_N_KEYS, :]
                act = 0.5 * a * (1.0 + lax.erf(a * (0.5 ** 0.5)))
                gate = jnp.zeros((PEER_N_KEYS, tb), BF16)
                for h in range(PEER_HEADS):
                    n1b = jnp.broadcast_to(n1_rows[h][i:i + 1, :], (PEER_N_KEYS, tb)).astype(BF16)
                    e1b = jnp.broadcast_to(e1_rows[h][i:i + 1, :], (PEER_N_KEYS, tb)).astype(BF16)
                    gate = gate + jnp.where(rank2_ref[h] < n1b, e1b * e2_ref[h], jnp.zeros_like(gate))
                w.append(gate * act.astype(BF16))
            outt_ref[...] += jnp.dot(vt_ref[:, c * W_CHUNK:(c + 1) * W_CHUNK], jnp.concatenate(w, axis=0),
                                     preferred_element_type=F32)

    @pl.when(j % 2 == 0)
    def _even():
        step(a0_ref, a1_ref)

    @pl.when(j % 2 == 1)
    def _odd():
        step(a1_ref, a0_ref)

    @pl.when(j == pl.num_programs(1) - 1)
    def _epilogue():
        h = h_ref[...]
        y_o[...] = _layer_norm(alpha * h + outt_ref[...].T, g_ref[...], b_ref[...])


def _peer(h, w, tb, alpha):
    n = h.shape[0]
    n_tiles = w["u"].shape[0] // PEER_TE
    weights_pre = (w["wq_t"], w["keys"])
    return pl.pallas_call(
        functools.partial(_peer_kernel, tb=tb, alpha=alpha),
        grid=(n // tb, n_tiles + 1),
        in_specs=[pl.BlockSpec((tb, D_MODEL), lambda i, j: (i, 0))]
        + [_full(a.shape) for a in weights_pre]
        + [pl.BlockSpec((PEER_TE, D_MODEL), lambda i, j: (jnp.minimum(j, n_tiles - 1), 0)),
           pl.BlockSpec((D_MODEL, PEER_TE), lambda i, j: (0, jnp.maximum(j - 1, 0))),
           _full(w["ln2_g"].shape), _full(w["ln2_b"].shape)],
        out_specs=pl.BlockSpec((tb, D_MODEL), lambda i, j: (i, 0)),
        out_shape=jax.ShapeDtypeStruct((n, D_MODEL), F32),
        scratch_shapes=[
            pltpu.VMEM((D_MODEL, tb), BF16),
            pltpu.VMEM((2 * PEER_HEADS * PEER_HALF, tb), F32),
            pltpu.VMEM((PEER_HEADS, PEER_N_KEYS, tb), F32),
            pltpu.VMEM((PEER_HEADS, PEER_N_KEYS, tb), F32),
            pltpu.VMEM((PEER_HEADS, PEER_N_KEYS, tb), BF16),
            pltpu.VMEM((PEER_HEADS, PEER_N_KEYS, tb), BF16),
            pltpu.VMEM((2, PEER_N_KEYS, tb), F32),
            pltpu.VMEM((CAND_ROWS, tb), F32),
            pltpu.VMEM((PEER_TE, tb), F32),
            pltpu.VMEM((PEER_TE, tb), F32),
            pltpu.VMEM((D_MODEL, tb), F32),
        ],
        compiler_params=_cparams(("parallel", "arbitrary")),
        name="peer",
    )(h, *weights_pre, w["u"], w["v_t"], w["ln2_g"], w["ln2_b"])


def _rope_tables(pos):
    inv = ROPE_THETA ** (-jnp.arange(ROPE_HALF, dtype=F32) / ROPE_HALF)
    ang = pos.astype(F32)[:, None] * inv[None, :]
    cos, sin = jnp.cos(ang), jnp.sin(ang)
    n = pos.shape[0]
    z16 = jnp.zeros((n, ROPE_HALF), F32)
    head = lambda fill: jnp.full((n, MLA_NOPE_DIM), fill, F32)
    tail = jnp.zeros((n, LANES - MLA_NOPE_DIM - MLA_ROPE_DIM), F32)
    cq = jnp.concatenate([head(1.0), cos, cos, tail], axis=1)
    ck = jnp.concatenate([head(0.0), cos, cos, tail], axis=1)
    s1 = jnp.concatenate([head(0.0), -sin, z16, tail], axis=1)
    s2 = jnp.concatenate([head(0.0), z16, sin, tail], axis=1)
    return cq, ck, s1, s2


def _pad_heads(wm, width):
    k = wm.shape[0]
    wm = wm.reshape(k, MLA_HEADS, width)
    return jnp.pad(wm, ((0, 0), (0, 0), (0, HEAD_PAD - width))).reshape(k, MLA_HEADS * HEAD_PAD)


def _prep_weights(w_in, q_norm_g, w_uq, kv_norm_g, w_uk, w_uv, w_oa, w_ob, w_out, ln1_g, ln1_b,
                  peer_wq, peer_sub_keys, peer_u, peer_v, ln2_g, ln2_b):
    o = 3 * SB_WIDTH + MLA_Q_LORA + MLA_KV_LORA
    w_kr = jnp.pad(w_in[:, o:o + MLA_ROPE_DIM], ((0, 0), (MLA_NOPE_DIM, LANES - MLA_NOPE_DIM - MLA_ROPE_DIM)))
    place = np.zeros((MLA_ROPE_DIM, LANES), np.float32)
    place[np.arange(MLA_ROPE_DIM), MLA_NOPE_DIM + np.arange(MLA_ROPE_DIM)] = 1.0
    og = o + MLA_ROPE_DIM
    row = lambda a: a.reshape(1, -1).astype(F32)
    return {
        "w_all": jnp.concatenate([w_in[:, :o], w_kr], axis=1).astype(BF16),
        "q_g": row(q_norm_g), "kv_g": row(kv_norm_g),
        "w_uq": _pad_heads(w_uq, MLA_NOPE_DIM + MLA_ROPE_DIM).astype(BF16),
        "w_uk": _pad_heads(w_uk, MLA_NOPE_DIM).astype(BF16),
        "w_uv": w_uv.astype(BF16),
        "place": jnp.asarray(place, dtype=BF16),
        "w_ga": w_in[:, og:og + D_MODEL].astype(BF16), "w_gb": w_in[:, og + D_MODEL:og + 2 * D_MODEL].astype(BF16),
        "w_oa": w_oa.astype(BF16), "w_ob": w_ob.astype(BF16), "w_out": w_out.astype(BF16),
        "ln1_g": row(ln1_g), "ln1_b": row(ln1_b), "ln2_g": row(ln2_g), "ln2_b": row(ln2_b),
        "wq_t": peer_wq.T.astype(BF16),
        "keys": peer_sub_keys.reshape(2 * PEER_HEADS, PEER_N_KEYS, PEER_HALF).astype(BF16),
        "u": peer_u.astype(BF16),
        "v_t": peer_v.T.astype(BF16),
    }


def _tile(n, pref):
    return pref if n % pref == 0 else n


def _layer(x, pos, past, w, alpha):
    b, t, _ = x.shape
    n = b * t
    tabs = _rope_tables(pos)
    if t % TOKEN_TILE == 0:
        tm = TOKEN_TILE
    elif TOKEN_TILE % t == 0 and n % TOKEN_TILE == 0:
        tm = TOKEN_TILE
        tabs = tuple(jnp.tile(a, (TOKEN_TILE // t, 1)) for a in tabs)
    else:
        tm = t
    xf = x.reshape(n, D_MODEL)
    sbq, sbk, sbv, sbkb, sbvb, ckv, kpe, qcat, kcat, vm = _proj(xf, tabs, w, tm)
    r3 = lambda a: a.reshape(b, t, a.shape[-1])
    q_sb, k_sb, v_sb, q_m, k_m, v_m = map(r3, (sbq, sbkb, sbvb, qcat, kcat, vm))
    if past is None:
        q_off, tk_valid = 0, t
    else:
        p_len = past[0].shape[1]
        flat = lambda a: a.reshape((b * p_len,) + a.shape[2:])
        pkc, pvm = _cache_prep(flat(past[2]), flat(past[3]), w, _tile(b * p_len, CACHE_TILE))
        p3 = lambda a: a.reshape(b, p_len, a.shape[-1])
        k_sb = jnp.concatenate([past[0].astype(BF16).reshape(b, p_len, SB_WIDTH), k_sb], axis=1)
        v_sb = jnp.concatenate([past[1].astype(BF16).reshape(b, p_len, SB_WIDTH), v_sb], axis=1)
        k_m = jnp.concatenate([p3(pkc), k_m], axis=1)
        v_m = jnp.concatenate([p3(pvm), v_m], axis=1)
        q_off, tk_valid = p_len, p_len + t
    tq = _tile(t, Q_TILE)
    tk = K_TILE
    pad = (-k_sb.shape[1]) % tk
    if pad:
        padk = lambda a: jnp.pad(a, ((0, 0), (0, pad), (0, 0)))
        k_sb, v_sb, k_m, v_m = map(padk, (k_sb, v_sb, k_m, v_m))
    o_a = _sb_attention(q_sb, k_sb, v_sb, q_off, tq, tk)
    o_b = _mla_attention(q_m, k_m, v_m, q_off, tk_valid, tq, tk)
    h = _merge(xf, o_a.reshape(n, SB_WIDTH), o_b.reshape(n, MLA_WIDTH), w, tm, alpha)
    y = _peer(h, w, _tile(n, TOKEN_TILE), alpha)
    state = (sbk.reshape(b, t, SB_HEADS, SB_HEAD_DIM), sbv.reshape(b, t, SB_HEADS, SB_HEAD_DIM),
             ckv.reshape(b, t, MLA_KV_LORA), kpe.reshape(b, t, MLA_ROPE_DIM))
    return y.reshape(b, t, D_MODEL), state


def kernel(x_prompt, x_sample, cache_sb_k, cache_sb_v, cache_mla_ckv, cache_mla_kpe, w_in, q_norm_g, w_uq,
           kv_norm_g, w_uk, w_uv, w_oa, w_ob, w_out, ln1_g, ln1_b, peer_wq, peer_sub_keys, peer_u, peer_v,
           ln2_g, ln2_b):
    depth = w_in.shape[0]
    alpha = (2.0 * depth) ** 0.25
    past_len = cache_sb_k.shape[2]
    pos_p = jnp.arange(x_prompt.shape[1], dtype=jnp.int32)
    pos_s = past_len + jnp.arange(x_sample.shape[1], dtype=jnp.int32)
    y_p, y_s = x_prompt, x_sample
    st_p, st_s = [], []
    for l in range(depth):
        w = _prep_weights(w_in[l], q_norm_g[l], w_uq[l], kv_norm_g[l], w_uk[l], w_uv[l], w_oa[l], w_ob[l],
                          w_out[l], ln1_g[l], ln1_b[l], peer_wq[l], peer_sub_keys[l], peer_u[l], peer_v[l],
                          ln2_g[l], ln2_b[l])
        y_p, sp = _layer(y_p, pos_p, None, w, alpha)
        y_s, ss = _layer(y_s, pos_s, (cache_sb_k[l], cache_sb_v[l], cache_mla_ckv[l], cache_mla_kpe[l]), w, alpha)
        st_p.append(sp)
        st_s.append(ss)
    stack = lambda sts, i: jnp.stack([s[i] for s in sts])
    return (y_p, y_s, stack(st_p, 0), stack(st_p, 1), stack(st_p, 2), stack(st_p, 3),
            stack(st_s, 0), stack(st_s, 1), stack(st_s, 2), stack(st_s, 3))
```

```python
import functools

import numpy as np
import jax
import jax.numpy as jnp
from jax import lax
from jax.experimental import pallas as pl
from jax.experimental.pallas import tpu as pltpu

F32 = jnp.float32
BF16 = jnp.bfloat16

D_MODEL = 1024
CHUNK = 64
SB_HEADS = 8
SB_HEAD_DIM = 64
SB_WIDTH = SB_HEADS * SB_HEAD_DIM
MLA_HEADS = 8
MLA_NOPE_DIM = 64
MLA_ROPE_DIM = 32
MLA_V_DIM = 64
MLA_Q_LORA = 384
MLA_KV_LORA = 256
MLA_WIDTH = MLA_HEADS * MLA_V_DIM
ROPE_THETA = 10000.0
PEER_HEADS = 8
PEER_N_KEYS = 128
PEER_TOPK = 16
PEER_HALF = 128
EPS = 1e-5
NEG_INF = -1e30
LOG2_E = 1.4426950408889634
EXP_ZERO_F32 = -104.0

LANES = 128
HEAD_PAD = LANES
ROPE_HALF = MLA_ROPE_DIM // 2
VMEM_LIMIT = 56 * 2**20
TOKEN_TILE = 512
CACHE_TILE = 1024
Q_TILE = 512
K_TILE = 256
MLA_K_TILE = 512


def _cparams(sem):
    return pltpu.CompilerParams(dimension_semantics=sem, vmem_limit_bytes=VMEM_LIMIT)


def _full(shape):
    nd = len(shape)
    return pl.BlockSpec(shape, lambda *_: (0,) * nd)


def _rms(x, g):
    ms = jnp.mean(x * x, axis=-1, keepdims=True)
    return x * lax.rsqrt(ms + EPS) * g


def _layer_norm(x, g, b):
    mu = jnp.mean(x, axis=-1, keepdims=True)
    xc = x - mu
    var = jnp.mean(xc * xc, axis=-1, keepdims=True)
    return xc * lax.rsqrt(var + EPS) * g + b


def _rope_lanes(x, c, s1, s2):
    return x * c + pltpu.roll(x, LANES - ROPE_HALF, 1) * s1 + pltpu.roll(x, ROPE_HALF, 1) * s2


W_ALL = 3 * SB_WIDTH + MLA_Q_LORA + MLA_KV_LORA + LANES


def _proj_kernel(x_ref, cq_ref, ck_ref, s1_ref, s2_ref, wall_ref, qg_ref, kvg_ref, wuq_ref, wuk_ref, wuv_ref,
                 sbq_o, sbk_o, sbv_o, sbkb_o, sbvb_o, ckv_o, kpe_o, qcat_o, kcat_o, vm_o):
    xb = x_ref[...].astype(BF16)
    proj = jnp.dot(xb, wall_ref[...], preferred_element_type=F32)
    sbq_o[...] = (proj[:, 0:SB_WIDTH] * (SB_HEAD_DIM ** -0.5)).astype(BF16)
    k = proj[:, SB_WIDTH:2 * SB_WIDTH]
    v = proj[:, 2 * SB_WIDTH:3 * SB_WIDTH]
    for h in range(SB_HEADS):
        sl = slice(h * SB_HEAD_DIM, (h + 1) * SB_HEAD_DIM)
        sbk_o[:, h, :] = k[:, sl]
        sbv_o[:, h, :] = v[:, sl]
    sbkb_o[...] = k.astype(BF16)
    sbvb_o[...] = v.astype(BF16)
    o = 3 * SB_WIDTH
    cqn = _rms(proj[:, o:o + MLA_Q_LORA], qg_ref[...])
    q = jnp.dot(cqn.astype(BF16), wuq_ref[...], preferred_element_type=F32)
    o += MLA_Q_LORA
    ckv = _rms(proj[:, o:o + MLA_KV_LORA], kvg_ref[...])
    ckv_o[...] = ckv
    o += MLA_KV_LORA
    cb = ckv.astype(BF16)
    kn = jnp.dot(cb, wuk_ref[...], preferred_element_type=F32)
    vm_o[...] = jnp.dot(cb, wuv_ref[...], preferred_element_type=F32).astype(BF16)
    s1 = s1_ref[...]
    s2 = s2_ref[...]
    kp = _rope_lanes(proj[:, o:o + LANES], ck_ref[...], s1, s2)
    kpe_o[...] = kp[:, MLA_NOPE_DIM:MLA_NOPE_DIM + MLA_ROPE_DIM]
    cq = cq_ref[...]
    for h in range(MLA_HEADS):
        sl = slice(h * HEAD_PAD, (h + 1) * HEAD_PAD)
        qcat_o[:, sl] = _rope_lanes(q[:, sl], cq, s1, s2).astype(BF16)
        kcat_o[:, sl] = (kn[:, sl] + kp).astype(BF16)


def _proj(x, tabs, w, tm):
    n = x.shape[0]
    rt = tabs[0].shape[0] // tm
    tok = lambda width: pl.BlockSpec((tm, width), lambda i: (i, 0))
    tab = pl.BlockSpec((tm, LANES), lambda i: (i % rt, 0))
    outs = [(SB_WIDTH, BF16), (SB_WIDTH, F32), (SB_WIDTH, F32), (SB_WIDTH, BF16), (SB_WIDTH, BF16),
            (MLA_KV_LORA, F32), (MLA_ROPE_DIM, F32), (MLA_HEADS * HEAD_PAD, BF16), (MLA_HEADS * HEAD_PAD, BF16),
            (MLA_WIDTH, BF16)]
    weights = (w["w_all"], w["q_g"], w["kv_g"], w["w_uq"], w["w_uk"], w["w_uv"])
    out_specs = [tok(wd) for wd, _ in outs]
    out_shape = [jax.ShapeDtypeStruct((n, wd), dt) for wd, dt in outs]
    for i in (1, 2):
        out_specs[i] = pl.BlockSpec((tm, SB_HEADS, SB_HEAD_DIM), lambda i: (i, 0, 0))
        out_shape[i] = jax.ShapeDtypeStruct((n, SB_HEADS, SB_HEAD_DIM), F32)
    return pl.pallas_call(
        _proj_kernel,
        grid=(n // tm,),
        in_specs=[tok(D_MODEL), tab, tab, tab, tab] + [_full(a.shape) for a in weights],
        out_specs=out_specs,
        out_shape=out_shape,
        compiler_params=_cparams(("parallel",)),
        name="proj",
    )(x, *tabs, *weights)


def _cache_kernel(ckv_ref, kpe_ref, wuk_ref, wuv_ref, place_ref, kcat_o, vm_o):
    cb = ckv_ref[...].astype(BF16)
    kn = jnp.dot(cb, wuk_ref[...], preferred_element_type=F32)
    vm_o[...] = jnp.dot(cb, wuv_ref[...], preferred_element_type=F32).astype(BF16)
    kp = jnp.dot(kpe_ref[...].astype(BF16), place_ref[...], preferred_element_type=F32)
    for h in range(MLA_HEADS):
        sl = slice(h * HEAD_PAD, (h + 1) * HEAD_PAD)
        kcat_o[:, sl] = (kn[:, sl] + kp).astype(BF16)


def _cache_prep(ckv, kpe, w, tm):
    n = ckv.shape[0]
    tok = lambda width: pl.BlockSpec((tm, width), lambda i: (i, 0))
    outs = [(MLA_HEADS * HEAD_PAD, BF16), (MLA_WIDTH, BF16)]
    weights = (w["w_uk"], w["w_uv"], w["place"])
    return pl.pallas_call(
        _cache_kernel,
        grid=(n // tm,),
        in_specs=[tok(MLA_KV_LORA), tok(MLA_ROPE_DIM)] + [_full(a.shape) for a in weights],
        out_specs=[tok(wd) for wd, _ in outs],
        out_shape=[jax.ShapeDtypeStruct((n, wd), dt) for wd, dt in outs],
        compiler_params=_cparams(("parallel",)),
        name="cache_prep",
    )(ckv, kpe, *weights)


def _loop(n, step):
    def one(i, carry):
        step(i)
        return carry

    lax.fori_loop(0, n, one, 0)


def _sb_kernel(q_ref, k_ref, v_ref, tri_ref, o_ref, acc_ref, c_ref, *, tq, tk, nk, q_off):
    qi = pl.program_id(2)
    q = q_ref[0]
    lane = lax.broadcasted_iota(jnp.int32, (tq, LANES), 1)
    zero = jnp.zeros_like(q)
    qs = (jnp.where(lane < SB_HEAD_DIM, q, zero), jnp.where(lane >= SB_HEAD_DIM, q, zero))
    q_lo = q_off + qi * tq
    nkb = jnp.minimum(nk, (q_lo + tq - 1 + tk - 1) // tk)
    kb_diag = jnp.minimum(q_lo // tk, nkb)
    acc_ref[...] = jnp.zeros_like(acc_ref)
    c_ref[...] = jnp.zeros_like(c_ref)
    q_pos = lax.broadcasted_iota(jnp.int32, (tk, tq), 1) + q_lo
    krow = lax.broadcasted_iota(jnp.int32, (tk, tq), 0)
    tri = tri_ref[...]

    def blocks(kbs, masked):
        k0s = [pl.multiple_of(kb * tk, tk) for kb in kbs]
        zs = [[lax.dot_general(k_ref[0, pl.ds(k0, tk), :], qs[h], (((1,), (1,)), ((), ())),
                               preferred_element_type=F32) for h in range(2)] for k0 in k0s]
        stage = []
        for k0, z2 in zip(k0s, zs):
            causal = (krow + k0) < q_pos if masked else None
            for z in z2:
                lb = jnp.minimum(z, 0.0) - jnp.log(1.0 + jnp.exp(-jnp.abs(z)))
                lom = lb - z
                if masked:
                    lom = jnp.where(causal, lom, 0.0)
                hi = lom.astype(BF16)
                lo = (lom - hi.astype(F32)).astype(BF16)
                suf = jnp.dot(tri, jnp.concatenate([hi, lo], axis=0), preferred_element_type=F32)
                stage.append((lb, suf, suf[0:1, :] + lom[0:1, :], causal))
        for b, k0 in enumerate(k0s):
            v = v_ref[0, pl.ds(k0, tk), :]
            for h in range(2):
                lb, suf, tot, causal = stage[2 * b + h]
                c = c_ref[h, 0:1, :]
                a = jnp.exp(lb + suf + c)
                if masked:
                    a = jnp.where(causal, a, 0.0)
                acc_ref[h] += lax.dot_general(v, a.astype(BF16), (((0,), (0,)), ((), ())),
                                              preferred_element_type=F32)
                c_ref[h] = jnp.broadcast_to(c + tot, (8, tq))

    n_diag = nkb - kb_diag
    _loop(n_diag // 2, lambda i: blocks((nkb - 1 - 2 * i, nkb - 2 - 2 * i), True))
    _loop(n_diag % 2, lambda _: blocks((kb_diag,), True))

    def live():
        return (jnp.max(c_ref[...]) > EXP_ZERO_F32).astype(jnp.int32)

    def more(state):
        return jnp.logical_and(state[0] < kb_diag, state[1] > 0)

    def back(state):
        blocks((kb_diag - 1 - state[0],), False)
        return state[0] + 1, live()

    lax.while_loop(more, back, (jnp.int32(0), live()))
    sub = lax.broadcasted_iota(jnp.int32, (LANES, tq), 0)
    o_ref[0] = jnp.where(sub < SB_HEAD_DIM, acc_ref[0], acc_ref[1]).T.astype(BF16)


def _suffix_matrix(tk):
    s = np.arange(tk)[:, None]
    j = np.arange(tk)[None, :]
    tri = (j > s).astype(np.float32)
    return jnp.asarray(np.concatenate([tri, tri], axis=1), dtype=BF16)


def _sb_attention(q, k, v, q_off, tq, tk):
    b, t_q, _ = q.shape
    t_k = k.shape[1]
    nk = t_k // tk
    kern = functools.partial(_sb_kernel, tq=tq, tk=tk, nk=nk, q_off=q_off)
    kv_spec = pl.BlockSpec((1, t_k, LANES), lambda bi, p, qi: (bi, 0, p))
    return pl.pallas_call(
        kern,
        grid=(b, SB_WIDTH // LANES, t_q // tq),
        in_specs=[pl.BlockSpec((1, tq, LANES), lambda bi, p, qi: (bi, qi, p)), kv_spec, kv_spec,
                  _full((tk, 2 * tk))],
        out_specs=pl.BlockSpec((1, tq, LANES), lambda bi, p, qi: (bi, qi, p)),
        out_shape=jax.ShapeDtypeStruct((b, t_q, SB_WIDTH), BF16),
        scratch_shapes=[pltpu.VMEM((2, LANES, tq), F32), pltpu.VMEM((2, 8, tq), F32)],
        compiler_params=_cparams(("parallel", "parallel", "arbitrary")),
        name="sb_attention",
    )(q, k, v, _suffix_matrix(tk))


def _mla_kernel(qa_ref, qb_ref, ka_ref, kb_ref, v_ref, o_ref, acc_ref, m_ref, l_ref, *, tq, tk, nk, q_off, tk_valid):
    qi = pl.program_id(2)
    q_lo = q_off + qi * tq
    k_hi = ((q_lo + tq - 1) // CHUNK + 1) * CHUNK
    nkb = jnp.minimum(nk, (k_hi + tk - 1) // tk)
    n_free = jnp.minimum(jnp.minimum(((q_lo // CHUNK + 1) * CHUNK) // tk, tk_valid // tk), nkb)
    acc_ref[...] = jnp.zeros_like(acc_ref)
    l_ref[...] = jnp.zeros_like(l_ref)
    m_ref[...] = jnp.full_like(m_ref, NEG_INF)
    q_chunk = (lax.broadcasted_iota(jnp.int32, (tk, tq), 1) + q_lo) // CHUNK
    krow = lax.broadcasted_iota(jnp.int32, (tk, tq), 0)
    c2 = (MLA_NOPE_DIM + MLA_ROPE_DIM) ** -0.5 * LOG2_E
    qs = (qa_ref[0], qb_ref[0])
    k_refs = (ka_ref, kb_ref)

    def blocks(kbs, masked):
        k0s = [pl.multiple_of(kb * tk, tk) for kb in kbs]
        raws = [[lax.dot_general(k_refs[h][0, pl.ds(k0, tk), :], qs[h], (((1,), (1,)), ((), ())),
                                 preferred_element_type=F32) for h in range(2)] for k0 in k0s]
        for k0, raw2 in zip(k0s, raws):
            v = v_ref[0, pl.ds(k0, tk), :]
            if masked:
                kidx = krow + k0
                mask = jnp.where(kidx < tk_valid, kidx // CHUNK, q_chunk + 1) <= q_chunk
            for h in range(2):
                raw = jnp.where(mask, raw2[h], NEG_INF) if masked else raw2[h]
                m_prev = m_ref[h, 0:1, :]
                m_next = jnp.maximum(m_prev, jnp.max(raw, axis=0, keepdims=True))
                p = jnp.exp2((raw - m_next) * c2)
                alpha = jnp.exp2((m_prev - m_next) * c2)
                l_next = alpha * l_ref[h, 0:1, :] + jnp.sum(p, axis=0, keepdims=True)
                pv = lax.dot_general(v, p.astype(BF16), (((0,), (0,)), ((), ())), preferred_element_type=F32)
                acc_ref[h] = alpha * acc_ref[h] + pv
                l_ref[h] = jnp.broadcast_to(l_next, (8, tq))
                m_ref[h] = jnp.broadcast_to(m_next, (8, tq))

    n4 = n_free // 4
    _loop(n4, lambda i: blocks(tuple(4 * i + d for d in range(4)), False))
    _loop((n_free % 4) // 2, lambda _: blocks((4 * n4, 4 * n4 + 1), False))
    _loop(n_free % 2, lambda _: blocks((n_free - 1,), False))
    n_edge = nkb - n_free
    _loop(n_edge // 2, lambda i: blocks((n_free + 2 * i, n_free + 2 * i + 1), True))
    _loop(n_edge % 2, lambda _: blocks((nkb - 1,), True))
    sub = lax.broadcasted_iota(jnp.int32, (LANES, tq), 0)
    out_t = jnp.where(sub < MLA_V_DIM, acc_ref[0] / l_ref[0, 0:1, :], acc_ref[1] / l_ref[1, 0:1, :])
    o_ref[0] = out_t.T.astype(BF16)


def _mla_attention(q, k, v, q_off, tk_valid, tq, tk):
    b, t_q, _ = q.shape
    t_k = k.shape[1]
    nk = t_k // tk
    kern = functools.partial(_mla_kernel, tq=tq, tk=tk, nk=nk, q_off=q_off, tk_valid=tk_valid)
    q_spec = lambda o: pl.BlockSpec((1, tq, LANES), lambda bi, p, qi: (bi, qi, 2 * p + o))
    k_spec = lambda o: pl.BlockSpec((1, t_k, LANES), lambda bi, p, qi: (bi, 0, 2 * p + o))
    return pl.pallas_call(
        kern,
        grid=(b, MLA_HEADS // 2, t_q // tq),
        in_specs=[q_spec(0), q_spec(1), k_spec(0), k_spec(1),
                  pl.BlockSpec((1, t_k, LANES), lambda bi, p, qi: (bi, 0, p))],
        out_specs=pl.BlockSpec((1, tq, LANES), lambda bi, p, qi: (bi, qi, p)),
        out_shape=jax.ShapeDtypeStruct((b, t_q, MLA_WIDTH), BF16),
        scratch_shapes=[pltpu.VMEM((2, LANES, tq), F32), pltpu.VMEM((2, 8, tq), F32), pltpu.VMEM((2, 8, tq), F32)],
        compiler_params=_cparams(("parallel", "parallel", "arbitrary")),
        name="mla_attention",
    )(q, q, k, k, v)


def _merge_kernel(x_ref, oa_ref, ob_ref, wga_ref, wgb_ref, woa_ref, wob_ref, wout_ref, g_ref, b_ref, h_o, *, alpha):
    x = x_ref[...]
    xb = x.astype(BF16)
    ga = jax.nn.sigmoid(jnp.dot(xb, wga_ref[...], preferred_element_type=F32))
    gb = jax.nn.sigmoid(jnp.dot(xb, wgb_ref[...], preferred_element_type=F32))
    merged = (ga * jnp.dot(oa_ref[...], woa_ref[...], preferred_element_type=F32)
              + gb * jnp.dot(ob_ref[...], wob_ref[...], preferred_element_type=F32))
    y = alpha * x + jnp.dot(merged.astype(BF16), wout_ref[...], preferred_element_type=F32)
    h_o[...] = _layer_norm(y, g_ref[...], b_ref[...])


def _merge(x, oa, ob, w, tm, alpha):
    n = x.shape[0]
    tok = lambda width: pl.BlockSpec((tm, width), lambda i: (i, 0))
    weights = (w["w_ga"], w["w_gb"], w["w_oa"], w["w_ob"], w["w_out"], w["ln1_g"], w["ln1_b"])
    return pl.pallas_call(
        functools.partial(_merge_kernel, alpha=alpha),
        grid=(n // tm,),
        in_specs=[tok(D_MODEL), tok(SB_WIDTH), tok(MLA_WIDTH)] + [_full(a.shape) for a in weights],
        out_specs=tok(D_MODEL),
        out_shape=jax.ShapeDtypeStruct((n, D_MODEL), F32),
        compiler_params=_cparams(("parallel",)),
        name="merge",
    )(x, oa, ob, *weights)


PEER_TE = 1024
I1_PER_TILE = PEER_TE // PEER_N_KEYS
W_CHUNK = 512
CAND_COUNTS = tuple(PEER_TOPK // (a + 1) for a in range(PEER_TOPK))
CAND_STARTS = tuple(int(x) for x in np.cumsum((0,) + CAND_COUNTS[:-1]))
N_CAND = sum(CAND_COUNTS)
CAND_ROWS = -(-N_CAND // 8) * 8


def _extract_rounds(s_refs, n_rows, tb):
    row = lax.broadcasted_iota(jnp.int32, (n_rows, tb), 0).astype(F32)
    slot = lax.broadcasted_iota(jnp.int32, (PEER_TOPK, tb), 0)

    def body(r, carry):
        out = []
        for s_ref, (vals, firsts) in zip(s_refs, carry):
            s = s_ref[...]
            m = jnp.max(s, axis=0, keepdims=True)
            first = jnp.min(jnp.where(s == m, row, float(n_rows)), axis=0, keepdims=True)
            s_ref[...] = jnp.where(row == first, -jnp.inf, s)
            out.append((jnp.where(slot == r, m, vals), jnp.where(slot == r, first, firsts)))
        return tuple(out)

    init = tuple((jnp.zeros((PEER_TOPK, tb), F32), jnp.zeros((PEER_TOPK, tb), F32)) for _ in s_refs)
    return lax.fori_loop(0, PEER_TOPK, body, init)


def _remove_maxima(s_refs, tb):
    slot = lax.broadcasted_iota(jnp.int32, (PEER_TOPK, tb), 0)

    def body(r, carry):
        out = []
        for s_ref, vals in zip(s_refs, carry):
            s = s_ref[...]
            m = jnp.max(s, axis=0, keepdims=True)
            s_ref[...] = jnp.where(s == m, -jnp.inf, s)
            out.append(jnp.where(slot == r, m, vals))
        return tuple(out)

    return lax.fori_loop(0, PEER_TOPK, body, tuple(jnp.zeros((PEER_TOPK, tb), F32) for _ in s_refs))


def _removed_mismatch(s_after, expected):
    gone = jnp.where(s_after == -jnp.inf, 1.0, 0.0)
    per_col = jnp.sum(gone.reshape(-1, gone.shape[-1]), axis=0, keepdims=True)
    return (jnp.max(jnp.abs(per_col - float(expected))) > 0.0).astype(jnp.int32)


def _scatter_rows(firsts, values, n_rows, tb, fill):
    row = lax.broadcasted_iota(jnp.int32, (n_rows, tb), 0).astype(F32)
    out = jnp.full((n_rows, tb), fill, F32)
    for r in range(PEER_TOPK):
        out = jnp.where(row == firsts[r:r + 1, :], values[r], out)
    return out


def _peer_kernel(h_ref, wqt_ref, keys_ref, u_ref, vt_ref, g_ref, b_ref, y_o,
                 htb_ref, qt_ref, n1_ref, e1_ref, rank2_ref, e2_ref, work_ref, cand_ref, a0_ref, a1_ref, outt_ref,
                 *, tb, alpha):
    j = pl.program_id(1)

    @pl.when(j == 0)
    def _prologue():
        htb = h_ref[...].T.astype(BF16)
        htb_ref[...] = htb
        qt_ref[...] = jnp.dot(wqt_ref[...], htb, preferred_element_type=F32)
        outt_ref[...] = jnp.zeros_like(outt_ref)
        a1_ref[...] = jnp.zeros_like(a1_ref)

        def fill_candidates(t1, t2):
            cand_ref[...] = jnp.full_like(cand_ref, -jnp.inf)
            for a in range(PEER_TOPK):
                cand_ref[CAND_STARTS[a]:CAND_STARTS[a] + CAND_COUNTS[a], :] = t1[a:a + 1, :] + t2[0:CAND_COUNTS[a], :]

        def finish(h, s, t1, t2, top, n1, rank2):
            n1_ref[h] = n1
            rank2_ref[h] = rank2.astype(BF16)
            z = jnp.sum(jnp.exp(top - top[0:1, :]), axis=0, keepdims=True)
            e1_ref[h] = jnp.exp(s[0] - t1[0:1, :])
            e2_ref[h] = (jnp.exp(s[1] - t2[0:1, :]) / z).astype(BF16)

        def head_by_value(h, s):
            work_ref[0] = s[0]
            work_ref[1] = s[1]
            t1, t2 = _remove_maxima((work_ref.at[0], work_ref.at[1]), tb)
            bad = _removed_mismatch(work_ref[...], 2 * PEER_TOPK)
            fill_candidates(t1, t2)
            (top,) = _remove_maxima((cand_ref,), tb)
            gone = cand_ref[...] == -jnp.inf
            bad = jnp.maximum(bad, _removed_mismatch(cand_ref[...], PEER_TOPK + CAND_ROWS - N_CAND))
            n1 = jnp.zeros((PEER_N_KEYS, tb), F32)
            rank2 = jnp.full((PEER_N_KEYS, tb), float(PEER_TOPK), F32)
            for a in range(PEER_TOPK):
                cnt = jnp.sum(jnp.where(gone[CAND_STARTS[a]:CAND_STARTS[a] + CAND_COUNTS[a], :], 1.0, 0.0),
                              axis=0, keepdims=True)
                n1 = jnp.where(s[0] == t1[a:a + 1, :], cnt, n1)
                rank2 = jnp.where(s[1] == t2[a:a + 1, :], float(a), rank2)
            finish(h, s, t1, t2, top, n1, rank2)
            return bad

        def head_exact(h, s):
            work_ref[0] = s[0]
            work_ref[1] = s[1]
            (t1, first1), (t2, first2) = _extract_rounds((work_ref.at[0], work_ref.at[1]), PEER_N_KEYS, tb)
            fill_candidates(t1, t2)
            ((top, cfirst),) = _extract_rounds((cand_ref,), CAND_ROWS, tb)
            counts = [jnp.sum(jnp.where(cfirst >= float(CAND_STARTS[a]),
                                        jnp.where(cfirst < float(CAND_STARTS[a] + CAND_COUNTS[a]), 1.0, 0.0), 0.0),
                              axis=0, keepdims=True) for a in range(PEER_TOPK)]
            n1 = _scatter_rows(first1, counts, PEER_N_KEYS, tb, 0.0)
            rank2 = _scatter_rows(first2, [float(r) for r in range(PEER_TOPK)], PEER_N_KEYS, tb, float(PEER_TOPK))
            finish(h, s, t1, t2, top, n1, rank2)

        def per_head(h, carry):
            s = []
            for c in range(2):
                r0 = pl.multiple_of((2 * h + c) * PEER_HALF, PEER_HALF)
                q = qt_ref[pl.ds(r0, PEER_HALF), :].astype(BF16)
                s.append(jnp.dot(keys_ref[2 * h + c], q, preferred_element_type=F32))
            tie = head_by_value(h, s)

            @pl.when(tie > 0)
            def _():
                head_exact(h, s)

            return carry

        lax.fori_loop(0, PEER_HEADS, per_head, 0)

    i1_0 = pl.multiple_of(jnp.maximum(j - 1, 0) * I1_PER_TILE, I1_PER_TILE)

    def step(a_next_ref, a_prev_ref):
        a_next_ref[...] = jnp.dot(u_ref[...], htb_ref[...], preferred_element_type=F32)
        n1_rows = [n1_ref[h, pl.ds(i1_0, I1_PER_TILE), :] for h in range(PEER_HEADS)]
        e1_rows = [e1_ref[h, pl.ds(i1_0, I1_PER_TILE), :] for h in range(PEER_HEADS)]
        for c in range(PEER_TE // W_CHUNK):
            w = []
            for i in range(c * W_CHUNK // PEER_N_KEYS, (c + 1) * W_CHUNK // PEER_N_KEYS):
                a = a_prev_ref[i * PEER_N_KEYS:(i + 1) * PEER_N_KEYS, :]
                act = 0.5 * a * (1.0 + lax.erf(a * (0.5 ** 0.5)))
                gate = jnp.zeros((PEER_N_KEYS, tb), BF16)
                for h in range(PEER_HEADS):
                    n1b = jnp.broadcast_to(n1_rows[h][i:i + 1, :], (PEER_N_KEYS, tb)).astype(BF16)
                    e1b = jnp.broadcast_to(e1_rows[h][i:i + 1, :], (PEER_N_KEYS, tb)).astype(BF16)
                    gate = gate + jnp.where(rank2_ref[h] < n1b, e1b * e2_ref[h], jnp.zeros_like(gate))
                w.append(gate * act.astype(BF16))
            outt_ref[...] += jnp.dot(vt_ref[:, c * W_CHUNK:(c + 1) * W_CHUNK], jnp.concatenate(w, axis=0),
                                     preferred_element_type=F32)

    @pl.when(j % 2 == 0)
    def _even():
        step(a0_ref, a1_ref)

    @pl.when(j % 2 == 1)
    def _odd():
        step(a1_ref, a0_ref)

    @pl.when(j == pl.num_programs(1) - 1)
    def _epilogue():
        h = h_ref[...]
        y_o[...] = _layer_norm(alpha * h + outt_ref[...].T, g_ref[...], b_ref[...])


def _peer(h, w, tb, alpha):
    n = h.shape[0]
    n_tiles = w["u"].shape[0] // PEER_TE
    weights_pre = (w["wq_t"], w["keys"])
    return pl.pallas_call(
        functools.partial(_peer_kernel, tb=tb, alpha=alpha),
        grid=(n // tb, n_tiles + 1),
        in_specs=[pl.BlockSpec((tb, D_MODEL), lambda i, j: (i, 0))]
        + [_full(a.shape) for a in weights_pre]
        + [pl.BlockSpec((PEER_TE, D_MODEL), lambda i, j: (jnp.minimum(j, n_tiles - 1), 0)),
           pl.BlockSpec((D_MODEL, PEER_TE), lambda i, j: (0, jnp.maximum(j - 1, 0))),
           _full(w["ln2_g"].shape), _full(w["ln2_b"].shape)],
        out_specs=pl.BlockSpec((tb, D_MODEL), lambda i, j: (i, 0)),
        out_shape=jax.ShapeDtypeStruct((n, D_MODEL), F32),
        scratch_shapes=[
            pltpu.VMEM((D_MODEL, tb), BF16),
            pltpu.VMEM((2 * PEER_HEADS * PEER_HALF, tb), F32),
            pltpu.VMEM((PEER_HEADS, PEER_N_KEYS, tb), F32),
            pltpu.VMEM((PEER_HEADS, PEER_N_KEYS, tb), F32),
            pltpu.VMEM((PEER_HEADS, PEER_N_KEYS, tb), BF16),
            pltpu.VMEM((PEER_HEADS, PEER_N_KEYS, tb), BF16),
            pltpu.VMEM((2, PEER_N_KEYS, tb), F32),
            pltpu.VMEM((CAND_ROWS, tb), F32),
            pltpu.VMEM((PEER_TE, tb), F32),
            pltpu.VMEM((PEER_TE, tb), F32),
            pltpu.VMEM((D_MODEL, tb), F32),
        ],
        compiler_params=_cparams(("parallel", "arbitrary")),
        name="peer",
    )(h, *weights_pre, w["u"], w["v_t"], w["ln2_g"], w["ln2_b"])


def _rope_tables(pos):
    inv = ROPE_THETA ** (-jnp.arange(ROPE_HALF, dtype=F32) / ROPE_HALF)
    ang = pos.astype(F32)[:, None] * inv[None, :]
    cos, sin = jnp.cos(ang), jnp.sin(ang)
    n = pos.shape[0]
    z16 = jnp.zeros((n, ROPE_HALF), F32)
    head = lambda fill: jnp.full((n, MLA_NOPE_DIM), fill, F32)
    tail = jnp.zeros((n, LANES - MLA_NOPE_DIM - MLA_ROPE_DIM), F32)
    cq = jnp.concatenate([head(1.0), cos, cos, tail], axis=1)
    ck = jnp.concatenate([head(0.0), cos, cos, tail], axis=1)
    s1 = jnp.concatenate([head(0.0), -sin, z16, tail], axis=1)
    s2 = jnp.concatenate([head(0.0), z16, sin, tail], axis=1)
    return cq, ck, s1, s2


def _pad_heads(wm, width):
    k = wm.shape[0]
    wm = wm.reshape(k, MLA_HEADS, width)
    return jnp.pad(wm, ((0, 0), (0, 0), (0, HEAD_PAD - width))).reshape(k, MLA_HEADS * HEAD_PAD)


def _prep_weights(w_in, q_norm_g, w_uq, kv_norm_g, w_uk, w_uv, w_oa, w_ob, w_out, ln1_g, ln1_b,
                  peer_wq, peer_sub_keys, peer_u, peer_v, ln2_g, ln2_b):
    o = 3 * SB_WIDTH + MLA_Q_LORA + MLA_KV_LORA
    w_kr = jnp.pad(w_in[:, o:o + MLA_ROPE_DIM], ((0, 0), (MLA_NOPE_DIM, LANES - MLA_NOPE_DIM - MLA_ROPE_DIM)))
    place = np.zeros((MLA_ROPE_DIM, LANES), np.float32)
    place[np.arange(MLA_ROPE_DIM), MLA_NOPE_DIM + np.arange(MLA_ROPE_DIM)] = 1.0
    og = o + MLA_ROPE_DIM
    row = lambda a: a.reshape(1, -1).astype(F32)
    return {
        "w_all": jnp.concatenate([w_in[:, :o], w_kr], axis=1).astype(BF16),
        "q_g": row(q_norm_g), "kv_g": row(kv_norm_g),
        "w_uq": _pad_heads(w_uq, MLA_NOPE_DIM + MLA_ROPE_DIM).astype(BF16),
        "w_uk": _pad_heads(w_uk, MLA_NOPE_DIM).astype(BF16),
        "w_uv": w_uv.astype(BF16),
        "place": jnp.asarray(place, dtype=BF16),
        "w_ga": w_in[:, og:og + D_MODEL].astype(BF16), "w_gb": w_in[:, og + D_MODEL:og + 2 * D_MODEL].astype(BF16),
        "w_oa": w_oa.astype(BF16), "w_ob": w_ob.astype(BF16), "w_out": w_out.astype(BF16),
        "ln1_g": row(ln1_g), "ln1_b": row(ln1_b), "ln2_g": row(ln2_g), "ln2_b": row(ln2_b),
        "wq_t": peer_wq.T.astype(BF16),
        "keys": peer_sub_keys.reshape(2 * PEER_HEADS, PEER_N_KEYS, PEER_HALF).astype(BF16),
        "u": peer_u.astype(BF16),
        "v_t": peer_v.T.astype(BF16),
    }


def _tile(n, pref):
    return pref if n % pref == 0 else n


def _layer(x, pos, past, w, alpha):
    b, t, _ = x.shape
    n = b * t
    tabs = _rope_tables(pos)
    if t % TOKEN_TILE == 0:
        tm = TOKEN_TILE
    elif TOKEN_TILE % t == 0 and n % TOKEN_TILE == 0:
        tm = TOKEN_TILE
        tabs = tuple(jnp.tile(a, (TOKEN_TILE // t, 1)) for a in tabs)
    else:
        tm = t
    xf = x.reshape(n, D_MODEL)
    sbq, sbk, sbv, sbkb, sbvb, ckv, kpe, qcat, kcat, vm = _proj(xf, tabs, w, tm)
    r3 = lambda a: a.reshape(b, t, a.shape[-1])
    q_sb, k_sb, v_sb, q_m, k_m, v_m = map(r3, (sbq, sbkb, sbvb, qcat, kcat, vm))
    if past is None:
        q_off, tk_valid = 0, t
    else:
        p_len = past[0].shape[1]
        flat = lambda a: a.reshape((b * p_len,) + a.shape[2:])
        pkc, pvm = _cache_prep(flat(past[2]), flat(past[3]), w, _tile(b * p_len, CACHE_TILE))
        p3 = lambda a: a.reshape(b, p_len, a.shape[-1])
        k_sb = jnp.concatenate([past[0].astype(BF16).reshape(b, p_len, SB_WIDTH), k_sb], axis=1)
        v_sb = jnp.concatenate([past[1].astype(BF16).reshape(b, p_len, SB_WIDTH), v_sb], axis=1)
        k_m = jnp.concatenate([p3(pkc), k_m], axis=1)
        v_m = jnp.concatenate([p3(pvm), v_m], axis=1)
        q_off, tk_valid = p_len, p_len + t
    tq = _tile(t, Q_TILE)
    pad_keys = lambda a, tk: jnp.pad(a, ((0, 0), (0, (-a.shape[1]) % tk), (0, 0)))
    o_a = _sb_attention(q_sb, pad_keys(k_sb, K_TILE), pad_keys(v_sb, K_TILE), q_off, tq, K_TILE)
    o_b = _mla_attention(q_m, pad_keys(k_m, MLA_K_TILE), pad_keys(v_m, MLA_K_TILE), q_off, tk_valid, tq, MLA_K_TILE)
    h = _merge(xf, o_a.reshape(n, SB_WIDTH), o_b.reshape(n, MLA_WIDTH), w, tm, alpha)
    y = _peer(h, w, _tile(n, TOKEN_TILE), alpha)
    state = (sbk.reshape(b, t, SB_HEADS, SB_HEAD_DIM), sbv.reshape(b, t, SB_HEADS, SB_HEAD_DIM),
             ckv.reshape(b, t, MLA_KV_LORA), kpe.reshape(b, t, MLA_ROPE_DIM))
    return y.reshape(b, t, D_MODEL), state


def kernel(x_prompt, x_sample, cache_sb_k, cache_sb_v, cache_mla_ckv, cache_mla_kpe, w_in, q_norm_g, w_uq,
           kv_norm_g, w_uk, w_uv, w_oa, w_ob, w_out, ln1_g, ln1_b, peer_wq, peer_sub_keys, peer_u, peer_v,
           ln2_g, ln2_b):
    depth = w_in.shape[0]
    alpha = (2.0 * depth) ** 0.25
    past_len = cache_sb_k.shape[2]
    pos_p = jnp.arange(x_prompt.shape[1], dtype=jnp.int32)
    pos_s = past_len + jnp.arange(x_sample.shape[1], dtype=jnp.int32)
    y_p, y_s = x_prompt, x_sample
    st_p, st_s = [], []
    for l in range(depth):
        w = _prep_weights(w_in[l], q_norm_g[l], w_uq[l], kv_norm_g[l], w_uk[l], w_uv[l], w_oa[l], w_ob[l],
                          w_out[l], ln1_g[l], ln1_b[l], peer_wq[l], peer_sub_keys[l], peer_u[l], peer_v[l],
                          ln2_g[l], ln2_b[l])
        y_p, sp = _layer(y_p, pos_p, None, w, alpha)
        y_s, ss = _layer(y_s, pos_s, (cache_sb_k[l], cache_sb_v[l], cache_mla_ckv[l], cache_mla_kpe[l]), w, alpha)
        st_p.append(sp)
        st_s.append(ss)
    stack = lambda sts, i: jnp.stack([s[i] for s in sts])
    return (y_p, y_s, stack(st_p, 0), stack(st_p, 1), stack(st_p, 2), stack(st_p, 3),
            stack(st_s, 0), stack(st_s, 1), stack(st_s, 2), stack(st_s, 3))
```

```python
import functools

import numpy as np
import jax
import jax.numpy as jnp
from jax import lax
from jax.experimental import pallas as pl
from jax.experimental.pallas import tpu as pltpu

F32 = jnp.float32
BF16 = jnp.bfloat16

D_MODEL = 1024
CHUNK = 64
SB_HEADS = 8
SB_HEAD_DIM = 64
SB_WIDTH = SB_HEADS * SB_HEAD_DIM
MLA_HEADS = 8
MLA_NOPE_DIM = 64
MLA_ROPE_DIM = 32
MLA_V_DIM = 64
MLA_Q_LORA = 384
MLA_KV_LORA = 256
MLA_WIDTH = MLA_HEADS * MLA_V_DIM
ROPE_THETA = 10000.0
PEER_HEADS = 8
PEER_N_KEYS = 128
PEER_TOPK = 16
PEER_HALF = 128
EPS = 1e-5
NEG_INF = -1e30
LOG2_E = 1.4426950408889634
EXP_ZERO_F32 = -104.0

LANES = 128
HEAD_PAD = LANES
ROPE_HALF = MLA_ROPE_DIM // 2
VMEM_LIMIT = 56 * 2**20
TOKEN_TILE = 512
CACHE_TILE = 1024
Q_TILE = 512
K_TILE = 256
MLA_K_TILE = 1024


def _cparams(sem):
    return pltpu.CompilerParams(dimension_semantics=sem, vmem_limit_bytes=VMEM_LIMIT)


def _full(shape):
    nd = len(shape)
    return pl.BlockSpec(shape, lambda *_: (0,) * nd)


def _rms(x, g):
    ms = jnp.mean(x * x, axis=-1, keepdims=True)
    return x * lax.rsqrt(ms + EPS) * g


def _layer_norm(x, g, b):
    mu = jnp.mean(x, axis=-1, keepdims=True)
    xc = x - mu
    var = jnp.mean(xc * xc, axis=-1, keepdims=True)
    return xc * lax.rsqrt(var + EPS) * g + b


def _rope_lanes(x, c, s1, s2):
    return x * c + pltpu.roll(x, LANES - ROPE_HALF, 1) * s1 + pltpu.roll(x, ROPE_HALF, 1) * s2


W_ALL = 3 * SB_WIDTH + MLA_Q_LORA + MLA_KV_LORA + LANES


def _proj_kernel(x_ref, cq_ref, ck_ref, s1_ref, s2_ref, wall_ref, qg_ref, kvg_ref, wuq_ref, wuk_ref, wuv_ref,
                 sbq_o, sbk_o, sbv_o, sbkb_o, sbvb_o, ckv_o, kpe_o, qcat_o, kcat_o, vm_o):
    xb = x_ref[...].astype(BF16)
    proj = jnp.dot(xb, wall_ref[...], preferred_element_type=F32)
    sbq_o[...] = (proj[:, 0:SB_WIDTH] * (SB_HEAD_DIM ** -0.5)).astype(BF16)
    k = proj[:, SB_WIDTH:2 * SB_WIDTH]
    v = proj[:, 2 * SB_WIDTH:3 * SB_WIDTH]
    for h in range(SB_HEADS):
        sl = slice(h * SB_HEAD_DIM, (h + 1) * SB_HEAD_DIM)
        sbk_o[:, h, :] = k[:, sl]
        sbv_o[:, h, :] = v[:, sl]
    sbkb_o[...] = k.astype(BF16)
    sbvb_o[...] = v.astype(BF16)
    o = 3 * SB_WIDTH
    cqn = _rms(proj[:, o:o + MLA_Q_LORA], qg_ref[...])
    q = jnp.dot(cqn.astype(BF16), wuq_ref[...], preferred_element_type=F32)
    o += MLA_Q_LORA
    ckv = _rms(proj[:, o:o + MLA_KV_LORA], kvg_ref[...])
    ckv_o[...] = ckv
    o += MLA_KV_LORA
    cb = ckv.astype(BF16)
    kn = jnp.dot(cb, wuk_ref[...], preferred_element_type=F32)
    vm_o[...] = jnp.dot(cb, wuv_ref[...], preferred_element_type=F32).astype(BF16)
    s1 = s1_ref[...]
    s2 = s2_ref[...]
    kp = _rope_lanes(proj[:, o:o + LANES], ck_ref[...], s1, s2)
    kpe_o[...] = kp[:, MLA_NOPE_DIM:MLA_NOPE_DIM + MLA_ROPE_DIM]
    cq = cq_ref[...]
    for h in range(MLA_HEADS):
        sl = slice(h * HEAD_PAD, (h + 1) * HEAD_PAD)
        qcat_o[:, sl] = _rope_lanes(q[:, sl], cq, s1, s2).astype(BF16)
        kcat_o[:, sl] = (kn[:, sl] + kp).astype(BF16)


def _proj(x, tabs, w, tm):
    n = x.shape[0]
    rt = tabs[0].shape[0] // tm
    tok = lambda width: pl.BlockSpec((tm, width), lambda i: (i, 0))
    tab = pl.BlockSpec((tm, LANES), lambda i: (i % rt, 0))
    outs = [(SB_WIDTH, BF16), (SB_WIDTH, F32), (SB_WIDTH, F32), (SB_WIDTH, BF16), (SB_WIDTH, BF16),
            (MLA_KV_LORA, F32), (MLA_ROPE_DIM, F32), (MLA_HEADS * HEAD_PAD, BF16), (MLA_HEADS * HEAD_PAD, BF16),
            (MLA_WIDTH, BF16)]
    weights = (w["w_all"], w["q_g"], w["kv_g"], w["w_uq"], w["w_uk"], w["w_uv"])
    out_specs = [tok(wd) for wd, _ in outs]
    out_shape = [jax.ShapeDtypeStruct((n, wd), dt) for wd, dt in outs]
    for i in (1, 2):
        out_specs[i] = pl.BlockSpec((tm, SB_HEADS, SB_HEAD_DIM), lambda i: (i, 0, 0))
        out_shape[i] = jax.ShapeDtypeStruct((n, SB_HEADS, SB_HEAD_DIM), F32)
    return pl.pallas_call(
        _proj_kernel,
        grid=(n // tm,),
        in_specs=[tok(D_MODEL), tab, tab, tab, tab] + [_full(a.shape) for a in weights],
        out_specs=out_specs,
        out_shape=out_shape,
        compiler_params=_cparams(("parallel",)),
        name="proj",
    )(x, *tabs, *weights)


def _cache_kernel(ckv_ref, kpe_ref, wuk_ref, wuv_ref, place_ref, kcat_o, vm_o):
    cb = ckv_ref[...].astype(BF16)
    kn = jnp.dot(cb, wuk_ref[...], preferred_element_type=F32)
    vm_o[...] = jnp.dot(cb, wuv_ref[...], preferred_element_type=F32).astype(BF16)
    kp = jnp.dot(kpe_ref[...].astype(BF16), place_ref[...], preferred_element_type=F32)
    for h in range(MLA_HEADS):
        sl = slice(h * HEAD_PAD, (h + 1) * HEAD_PAD)
        kcat_o[:, sl] = (kn[:, sl] + kp).astype(BF16)


def _cache_prep(ckv, kpe, w, tm):
    n = ckv.shape[0]
    tok = lambda width: pl.BlockSpec((tm, width), lambda i: (i, 0))
    outs = [(MLA_HEADS * HEAD_PAD, BF16), (MLA_WIDTH, BF16)]
    weights = (w["w_uk"], w["w_uv"], w["place"])
    return pl.pallas_call(
        _cache_kernel,
        grid=(n // tm,),
        in_specs=[tok(MLA_KV_LORA), tok(MLA_ROPE_DIM)] + [_full(a.shape) for a in weights],
        out_specs=[tok(wd) for wd, _ in outs],
        out_shape=[jax.ShapeDtypeStruct((n, wd), dt) for wd, dt in outs],
        compiler_params=_cparams(("parallel",)),
        name="cache_prep",
    )(ckv, kpe, *weights)


def _loop(n, step):
    def one(i, carry):
        step(i)
        return carry

    lax.fori_loop(0, n, one, 0)


def _sb_kernel(q_ref, k_ref, v_ref, tri_ref, o_ref, acc_ref, c_ref, *, tq, tk, nk, q_off):
    qi = pl.program_id(2)
    q = q_ref[0]
    lane = lax.broadcasted_iota(jnp.int32, (tq, LANES), 1)
    zero = jnp.zeros_like(q)
    qs = (jnp.where(lane < SB_HEAD_DIM, q, zero), jnp.where(lane >= SB_HEAD_DIM, q, zero))
    q_lo = q_off + qi * tq
    nkb = jnp.minimum(nk, (q_lo + tq - 1 + tk - 1) // tk)
    kb_diag = jnp.minimum(q_lo // tk, nkb)
    acc_ref[...] = jnp.zeros_like(acc_ref)
    c_ref[...] = jnp.zeros_like(c_ref)
    q_pos = lax.broadcasted_iota(jnp.int32, (tk, tq), 1) + q_lo
    krow = lax.broadcasted_iota(jnp.int32, (tk, tq), 0)
    tri = tri_ref[...]

    def blocks(kbs, masked):
        k0s = [pl.multiple_of(kb * tk, tk) for kb in kbs]
        zs = [[lax.dot_general(k_ref[0, pl.ds(k0, tk), :], qs[h], (((1,), (1,)), ((), ())),
                               preferred_element_type=F32) for h in range(2)] for k0 in k0s]
        stage = []
        for k0, z2 in zip(k0s, zs):
            causal = (krow + k0) < q_pos if masked else None
            for z in z2:
                lb = jnp.minimum(z, 0.0) - jnp.log(1.0 + jnp.exp(-jnp.abs(z)))
                lom = lb - z
                if masked:
                    lom = jnp.where(causal, lom, 0.0)
                hi = lom.astype(BF16)
                lo = (lom - hi.astype(F32)).astype(BF16)
                suf = jnp.dot(tri, jnp.concatenate([hi, lo], axis=0), preferred_element_type=F32)
                stage.append((lb, suf, suf[0:1, :] + lom[0:1, :], causal))
        for b, k0 in enumerate(k0s):
            v = v_ref[0, pl.ds(k0, tk), :]
            for h in range(2):
                lb, suf, tot, causal = stage[2 * b + h]
                c = c_ref[h, 0:1, :]
                a = jnp.exp(lb + suf + c)
                if masked:
                    a = jnp.where(causal, a, 0.0)
                acc_ref[h] += lax.dot_general(v, a.astype(BF16), (((0,), (0,)), ((), ())),
                                              preferred_element_type=F32)
                c_ref[h] = jnp.broadcast_to(c + tot, (8, tq))

    n_diag = nkb - kb_diag
    _loop(n_diag // 2, lambda i: blocks((nkb - 1 - 2 * i, nkb - 2 - 2 * i), True))
    _loop(n_diag % 2, lambda _: blocks((kb_diag,), True))

    def live():
        return (jnp.max(c_ref[...]) > EXP_ZERO_F32).astype(jnp.int32)

    def more(state):
        return jnp.logical_and(state[0] < kb_diag, state[1] > 0)

    def back(state):
        blocks((kb_diag - 1 - state[0],), False)
        return state[0] + 1, live()

    lax.while_loop(more, back, (jnp.int32(0), live()))
    sub = lax.broadcasted_iota(jnp.int32, (LANES, tq), 0)
    o_ref[0] = jnp.where(sub < SB_HEAD_DIM, acc_ref[0], acc_ref[1]).T.astype(BF16)


def _suffix_matrix(tk):
    s = np.arange(tk)[:, None]
    j = np.arange(tk)[None, :]
    tri = (j > s).astype(np.float32)
    return jnp.asarray(np.concatenate([tri, tri], axis=1), dtype=BF16)


def _sb_attention(q, k, v, q_off, tq, tk):
    b, t_q, _ = q.shape
    t_k = k.shape[1]
    nk = t_k // tk
    kern = functools.partial(_sb_kernel, tq=tq, tk=tk, nk=nk, q_off=q_off)
    kv_spec = pl.BlockSpec((1, t_k, LANES), lambda bi, p, qi: (bi, 0, p))
    return pl.pallas_call(
        kern,
        grid=(b, SB_WIDTH // LANES, t_q // tq),
        in_specs=[pl.BlockSpec((1, tq, LANES), lambda bi, p, qi: (bi, qi, p)), kv_spec, kv_spec,
                  _full((tk, 2 * tk))],
        out_specs=pl.BlockSpec((1, tq, LANES), lambda bi, p, qi: (bi, qi, p)),
        out_shape=jax.ShapeDtypeStruct((b, t_q, SB_WIDTH), BF16),
        scratch_shapes=[pltpu.VMEM((2, LANES, tq), F32), pltpu.VMEM((2, 8, tq), F32)],
        compiler_params=_cparams(("parallel", "parallel", "arbitrary")),
        name="sb_attention",
    )(q, k, v, _suffix_matrix(tk))


def _mla_kernel(qa_ref, qb_ref, ka_ref, kb_ref, v_ref, o_ref, acc_ref, m_ref, l_ref, *, tq, tk, nk, q_off, tk_valid):
    qi = pl.program_id(2)
    q_lo = q_off + qi * tq
    k_hi = ((q_lo + tq - 1) // CHUNK + 1) * CHUNK
    nkb = jnp.minimum(nk, (k_hi + tk - 1) // tk)
    n_free = jnp.minimum(jnp.minimum(((q_lo // CHUNK + 1) * CHUNK) // tk, tk_valid // tk), nkb)
    acc_ref[...] = jnp.zeros_like(acc_ref)
    l_ref[...] = jnp.zeros_like(l_ref)
    m_ref[...] = jnp.full_like(m_ref, NEG_INF)
    q_chunk = (lax.broadcasted_iota(jnp.int32, (tk, tq), 1) + q_lo) // CHUNK
    krow = lax.broadcasted_iota(jnp.int32, (tk, tq), 0)
    c2 = (MLA_NOPE_DIM + MLA_ROPE_DIM) ** -0.5 * LOG2_E
    qs = (qa_ref[0], qb_ref[0])
    k_refs = (ka_ref, kb_ref)

    def blocks(kbs, masked):
        k0s = [pl.multiple_of(kb * tk, tk) for kb in kbs]
        raws = [[lax.dot_general(k_refs[h][0, pl.ds(k0, tk), :], qs[h], (((1,), (1,)), ((), ())),
                                 preferred_element_type=F32) for h in range(2)] for k0 in k0s]
        for k0, raw2 in zip(k0s, raws):
            v = v_ref[0, pl.ds(k0, tk), :]
            if masked:
                kidx = krow + k0
                mask = jnp.where(kidx < tk_valid, kidx // CHUNK, q_chunk + 1) <= q_chunk
            for h in range(2):
                raw = jnp.where(mask, raw2[h], NEG_INF) if masked else raw2[h]
                m_prev = m_ref[h, 0:1, :]
                m_next = jnp.maximum(m_prev, jnp.max(raw, axis=0, keepdims=True))
                p = jnp.exp2((raw - m_next) * c2)
                alpha = jnp.exp2((m_prev - m_next) * c2)
                l_next = alpha * l_ref[h, 0:1, :] + jnp.sum(p, axis=0, keepdims=True)
                pv = lax.dot_general(v, p.astype(BF16), (((0,), (0,)), ((), ())), preferred_element_type=F32)
                acc_ref[h] = alpha * acc_ref[h] + pv
                l_ref[h] = jnp.broadcast_to(l_next, (8, tq))
                m_ref[h] = jnp.broadcast_to(m_next, (8, tq))

    n4 = n_free // 4
    _loop(n4, lambda i: blocks(tuple(4 * i + d for d in range(4)), False))
    _loop((n_free % 4) // 2, lambda _: blocks((4 * n4, 4 * n4 + 1), False))
    _loop(n_free % 2, lambda _: blocks((n_free - 1,), False))
    n_edge = nkb - n_free
    _loop(n_edge // 2, lambda i: blocks((n_free + 2 * i, n_free + 2 * i + 1), True))
    _loop(n_edge % 2, lambda _: blocks((nkb - 1,), True))
    sub = lax.broadcasted_iota(jnp.int32, (LANES, tq), 0)
    out_t = jnp.where(sub < MLA_V_DIM, acc_ref[0] / l_ref[0, 0:1, :], acc_ref[1] / l_ref[1, 0:1, :])
    o_ref[0] = out_t.T.astype(BF16)


def _mla_attention(q, k, v, q_off, tk_valid, tq, tk):
    b, t_q, _ = q.shape
    t_k = k.shape[1]
    nk = t_k // tk
    kern = functools.partial(_mla_kernel, tq=tq, tk=tk, nk=nk, q_off=q_off, tk_valid=tk_valid)
    q_spec = lambda o: pl.BlockSpec((1, tq, LANES), lambda bi, p, qi: (bi, qi, 2 * p + o))
    k_spec = lambda o: pl.BlockSpec((1, t_k, LANES), lambda bi, p, qi: (bi, 0, 2 * p + o))
    return pl.pallas_call(
        kern,
        grid=(b, MLA_HEADS // 2, t_q // tq),
        in_specs=[q_spec(0), q_spec(1), k_spec(0), k_spec(1),
                  pl.BlockSpec((1, t_k, LANES), lambda bi, p, qi: (bi, 0, p))],
        out_specs=pl.BlockSpec((1, tq, LANES), lambda bi, p, qi: (bi, qi, p)),
        out_shape=jax.ShapeDtypeStruct((b, t_q, MLA_WIDTH), BF16),
        scratch_shapes=[pltpu.VMEM((2, LANES, tq), F32), pltpu.VMEM((2, 8, tq), F32), pltpu.VMEM((2, 8, tq), F32)],
        compiler_params=_cparams(("parallel", "parallel", "arbitrary")),
        name="mla_attention",
    )(q, q, k, k, v)


def _merge_kernel(x_ref, oa_ref, ob_ref, wga_ref, wgb_ref, woa_ref, wob_ref, wout_ref, g_ref, b_ref, h_o, *, alpha):
    x = x_ref[...]
    xb = x.astype(BF16)
    ga = jax.nn.sigmoid(jnp.dot(xb, wga_ref[...], preferred_element_type=F32))
    gb = jax.nn.sigmoid(jnp.dot(xb, wgb_ref[...], preferred_element_type=F32))
    merged = (ga * jnp.dot(oa_ref[...], woa_ref[...], preferred_element_type=F32)
              + gb * jnp.dot(ob_ref[...], wob_ref[...], preferred_element_type=F32))
    y = alpha * x + jnp.dot(merged.astype(BF16), wout_ref[...], preferred_element_type=F32)
    h_o[...] = _layer_norm(y, g_ref[...], b_ref[...])


def _merge(x, oa, ob, w, tm, alpha):
    n = x.shape[0]
    tok = lambda width: pl.BlockSpec((tm, width), lambda i: (i, 0))
    weights = (w["w_ga"], w["w_gb"], w["w_oa"], w["w_ob"], w["w_out"], w["ln1_g"], w["ln1_b"])
    return pl.pallas_call(
        functools.partial(_merge_kernel, alpha=alpha),
        grid=(n // tm,),
        in_specs=[tok(D_MODEL), tok(SB_WIDTH), tok(MLA_WIDTH)] + [_full(a.shape) for a in weights],
        out_specs=tok(D_MODEL),
        out_shape=jax.ShapeDtypeStruct((n, D_MODEL), F32),
        compiler_params=_cparams(("parallel",)),
        name="merge",
    )(x, oa, ob, *weights)


PEER_TE = 1024
I1_PER_TILE = PEER_TE // PEER_N_KEYS
W_CHUNK = 512
CAND_COUNTS = tuple(PEER_TOPK // (a + 1) for a in range(PEER_TOPK))
CAND_STARTS = tuple(int(x) for x in np.cumsum((0,) + CAND_COUNTS[:-1]))
N_CAND = sum(CAND_COUNTS)
CAND_ROWS = -(-N_CAND // 8) * 8


def _extract_rounds(s_refs, n_rows, tb):
    row = lax.broadcasted_iota(jnp.int32, (n_rows, tb), 0).astype(F32)
    slot = lax.broadcasted_iota(jnp.int32, (PEER_TOPK, tb), 0)

    def body(r, carry):
        out = []
        for s_ref, (vals, firsts) in zip(s_refs, carry):
            s = s_ref[...]
            m = jnp.max(s, axis=0, keepdims=True)
            first = jnp.min(jnp.where(s == m, row, float(n_rows)), axis=0, keepdims=True)
            s_ref[...] = jnp.where(row == first, -jnp.inf, s)
            out.append((jnp.where(slot == r, m, vals), jnp.where(slot == r, first, firsts)))
        return tuple(out)

    init = tuple((jnp.zeros((PEER_TOPK, tb), F32), jnp.zeros((PEER_TOPK, tb), F32)) for _ in s_refs)
    return lax.fori_loop(0, PEER_TOPK, body, init)


def _remove_maxima(s_refs, tb):
    slot = lax.broadcasted_iota(jnp.int32, (PEER_TOPK, tb), 0)

    def body(r, carry):
        out = []
        for s_ref, vals in zip(s_refs, carry):
            s = s_ref[...]
            m = jnp.max(s, axis=0, keepdims=True)
            s_ref[...] = jnp.where(s == m, -jnp.inf, s)
            out.append(jnp.where(slot == r, m, vals))
        return tuple(out)

    return lax.fori_loop(0, PEER_TOPK, body, tuple(jnp.zeros((PEER_TOPK, tb), F32) for _ in s_refs))


def _removed_mismatch(s_after, expected):
    gone = jnp.where(s_after == -jnp.inf, 1.0, 0.0)
    per_col = jnp.sum(gone.reshape(-1, gone.shape[-1]), axis=0, keepdims=True)
    return (jnp.max(jnp.abs(per_col - float(expected))) > 0.0).astype(jnp.int32)


def _scatter_rows(firsts, values, n_rows, tb, fill):
    row = lax.broadcasted_iota(jnp.int32, (n_rows, tb), 0).astype(F32)
    out = jnp.full((n_rows, tb), fill, F32)
    for r in range(PEER_TOPK):
        out = jnp.where(row == firsts[r:r + 1, :], values[r], out)
    return out


def _peer_kernel(h_ref, wqt_ref, keys_ref, u_ref, vt_ref, g_ref, b_ref, y_o,
                 htb_ref, qt_ref, n1_ref, e1_ref, rank2_ref, e2_ref, work_ref, cand_ref, a0_ref, a1_ref, outt_ref,
                 *, tb, alpha):
    j = pl.program_id(1)

    @pl.when(j == 0)
    def _prologue():
        htb = h_ref[...].T.astype(BF16)
        htb_ref[...] = htb
        qt_ref[...] = jnp.dot(wqt_ref[...], htb, preferred_element_type=F32)
        outt_ref[...] = jnp.zeros_like(outt_ref)
        a1_ref[...] = jnp.zeros_like(a1_ref)

        def fill_candidates(t1, t2):
            cand_ref[...] = jnp.full_like(cand_ref, -jnp.inf)
            for a in range(PEER_TOPK):
                cand_ref[CAND_STARTS[a]:CAND_STARTS[a] + CAND_COUNTS[a], :] = t1[a:a + 1, :] + t2[0:CAND_COUNTS[a], :]

        def finish(h, s, t1, t2, top, n1, rank2):
            n1_ref[h] = n1
            rank2_ref[h] = rank2.astype(BF16)
            z = jnp.sum(jnp.exp(top - top[0:1, :]), axis=0, keepdims=True)
            e1_ref[h] = jnp.exp(s[0] - t1[0:1, :])
            e2_ref[h] = (jnp.exp(s[1] - t2[0:1, :]) / z).astype(BF16)

        def head_by_value(h, s):
            work_ref[0] = s[0]
            work_ref[1] = s[1]
            t1, t2 = _remove_maxima((work_ref.at[0], work_ref.at[1]), tb)
            bad = _removed_mismatch(work_ref[...], 2 * PEER_TOPK)
            fill_candidates(t1, t2)
            (top,) = _remove_maxima((cand_ref,), tb)
            gone = cand_ref[...] == -jnp.inf
            bad = jnp.maximum(bad, _removed_mismatch(cand_ref[...], PEER_TOPK + CAND_ROWS - N_CAND))
            n1 = jnp.zeros((PEER_N_KEYS, tb), F32)
            rank2 = jnp.full((PEER_N_KEYS, tb), float(PEER_TOPK), F32)
            for a in range(PEER_TOPK):
                cnt = jnp.sum(jnp.where(gone[CAND_STARTS[a]:CAND_STARTS[a] + CAND_COUNTS[a], :], 1.0, 0.0),
                              axis=0, keepdims=True)
                n1 = jnp.where(s[0] == t1[a:a + 1, :], cnt, n1)
                rank2 = jnp.where(s[1] == t2[a:a + 1, :], float(a), rank2)
            finish(h, s, t1, t2, top, n1, rank2)
            return bad

        def head_exact(h, s):
            work_ref[0] = s[0]
            work_ref[1] = s[1]
            (t1, first1), (t2, first2) = _extract_rounds((work_ref.at[0], work_ref.at[1]), PEER_N_KEYS, tb)
            fill_candidates(t1, t2)
            ((top, cfirst),) = _extract_rounds((cand_ref,), CAND_ROWS, tb)
            counts = [jnp.sum(jnp.where(cfirst >= float(CAND_STARTS[a]),
                                        jnp.where(cfirst < float(CAND_STARTS[a] + CAND_COUNTS[a]), 1.0, 0.0), 0.0),
                              axis=0, keepdims=True) for a in range(PEER_TOPK)]
            n1 = _scatter_rows(first1, counts, PEER_N_KEYS, tb, 0.0)
            rank2 = _scatter_rows(first2, [float(r) for r in range(PEER_TOPK)], PEER_N_KEYS, tb, float(PEER_TOPK))
            finish(h, s, t1, t2, top, n1, rank2)

        def per_head(h, carry):
            s = []
            for c in range(2):
                r0 = pl.multiple_of((2 * h + c) * PEER_HALF, PEER_HALF)
                q = qt_ref[pl.ds(r0, PEER_HALF), :].astype(BF16)
                s.append(jnp.dot(keys_ref[2 * h + c], q, preferred_element_type=F32))
            tie = head_by_value(h, s)

            @pl.when(tie > 0)
            def _():
                head_exact(h, s)

            return carry

        lax.fori_loop(0, PEER_HEADS, per_head, 0)

    i1_0 = pl.multiple_of(jnp.maximum(j - 1, 0) * I1_PER_TILE, I1_PER_TILE)

    def step(a_next_ref, a_prev_ref):
        a_next_ref[...] = jnp.dot(u_ref[...], htb_ref[...], preferred_element_type=F32)
        n1_rows = [n1_ref[h, pl.ds(i1_0, I1_PER_TILE), :] for h in range(PEER_HEADS)]
        e1_rows = [e1_ref[h, pl.ds(i1_0, I1_PER_TILE), :] for h in range(PEER_HEADS)]
        for c in range(PEER_TE // W_CHUNK):
            w = []
            for i in range(c * W_CHUNK // PEER_N_KEYS, (c + 1) * W_CHUNK // PEER_N_KEYS):
                a = a_prev_ref[i * PEER_N_KEYS:(i + 1) * PEER_N_KEYS, :]
                act = 0.5 * a * (1.0 + lax.erf(a * (0.5 ** 0.5)))
                gate = None
                for h in range(PEER_HEADS):
                    n1b = jnp.broadcast_to(n1_rows[h][i:i + 1, :], (PEER_N_KEYS, tb)).astype(BF16)
                    e1b = jnp.broadcast_to(e1_rows[h][i:i + 1, :], (PEER_N_KEYS, tb)).astype(BF16)
                    term = jnp.where(rank2_ref[h] < n1b, e1b * e2_ref[h], jnp.zeros((PEER_N_KEYS, tb), BF16))
                    gate = term if gate is None else gate + term
                w.append(gate * act.astype(BF16))
            outt_ref[...] += jnp.dot(vt_ref[:, c * W_CHUNK:(c + 1) * W_CHUNK], jnp.concatenate(w, axis=0),
                                     preferred_element_type=F32)

    @pl.when(j % 2 == 0)
    def _even():
        step(a0_ref, a1_ref)

    @pl.when(j % 2 == 1)
    def _odd():
        step(a1_ref, a0_ref)

    @pl.when(j == pl.num_programs(1) - 1)
    def _epilogue():
        h = h_ref[...]
        y_o[...] = _layer_norm(alpha * h + outt_ref[...].T, g_ref[...], b_ref[...])


def _peer(h, w, tb, alpha):
    n = h.shape[0]
    n_tiles = w["u"].shape[0] // PEER_TE
    weights_pre = (w["wq_t"], w["keys"])
    return pl.pallas_call(
        functools.partial(_peer_kernel, tb=tb, alpha=alpha),
        grid=(n // tb, n_tiles + 1),
        in_specs=[pl.BlockSpec((tb, D_MODEL), lambda i, j: (i, 0))]
        + [_full(a.shape) for a in weights_pre]
        + [pl.BlockSpec((PEER_TE, D_MODEL), lambda i, j: (jnp.minimum(j, n_tiles - 1), 0)),
           pl.BlockSpec((D_MODEL, PEER_TE), lambda i, j: (0, jnp.maximum(j - 1, 0))),
           _full(w["ln2_g"].shape), _full(w["ln2_b"].shape)],
        out_specs=pl.BlockSpec((tb, D_MODEL), lambda i, j: (i, 0)),
        out_shape=jax.ShapeDtypeStruct((n, D_MODEL), F32),
        scratch_shapes=[
            pltpu.VMEM((D_MODEL, tb), BF16),
            pltpu.VMEM((2 * PEER_HEADS * PEER_HALF, tb), F32),
            pltpu.VMEM((PEER_HEADS, PEER_N_KEYS, tb), F32),
            pltpu.VMEM((PEER_HEADS, PEER_N_KEYS, tb), F32),
            pltpu.VMEM((PEER_HEADS, PEER_N_KEYS, tb), BF16),
            pltpu.VMEM((PEER_HEADS, PEER_N_KEYS, tb), BF16),
            pltpu.VMEM((2, PEER_N_KEYS, tb), F32),
            pltpu.VMEM((CAND_ROWS, tb), F32),
            pltpu.VMEM((PEER_TE, tb), F32),
            pltpu.VMEM((PEER_TE, tb), F32),
            pltpu.VMEM((D_MODEL, tb), F32),
        ],
        compiler_params=_cparams(("parallel", "arbitrary")),
        name="peer",
    )(h, *weights_pre, w["u"], w["v_t"], w["ln2_g"], w["ln2_b"])


def _rope_tables(pos):
    inv = ROPE_THETA ** (-jnp.arange(ROPE_HALF, dtype=F32) / ROPE_HALF)
    ang = pos.astype(F32)[:, None] * inv[None, :]
    cos, sin = jnp.cos(ang), jnp.sin(ang)
    n = pos.shape[0]
    z16 = jnp.zeros((n, ROPE_HALF), F32)
    head = lambda fill: jnp.full((n, MLA_NOPE_DIM), fill, F32)
    tail = jnp.zeros((n, LANES - MLA_NOPE_DIM - MLA_ROPE_DIM), F32)
    cq = jnp.concatenate([head(1.0), cos, cos, tail], axis=1)
    ck = jnp.concatenate([head(0.0), cos, cos, tail], axis=1)
    s1 = jnp.concatenate([head(0.0), -sin, z16, tail], axis=1)
    s2 = jnp.concatenate([head(0.0), z16, sin, tail], axis=1)
    return cq, ck, s1, s2


def _pad_heads(wm, width):
    k = wm.shape[0]
    wm = wm.reshape(k, MLA_HEADS, width)
    return jnp.pad(wm, ((0, 0), (0, 0), (0, HEAD_PAD - width))).reshape(k, MLA_HEADS * HEAD_PAD)


def _prep_weights(w_in, q_norm_g, w_uq, kv_norm_g, w_uk, w_uv, w_oa, w_ob, w_out, ln1_g, ln1_b,
                  peer_wq, peer_sub_keys, peer_u, peer_v, ln2_g, ln2_b):
    o = 3 * SB_WIDTH + MLA_Q_LORA + MLA_KV_LORA
    w_kr = jnp.pad(w_in[:, o:o + MLA_ROPE_DIM], ((0, 0), (MLA_NOPE_DIM, LANES - MLA_NOPE_DIM - MLA_ROPE_DIM)))
    place = np.zeros((MLA_ROPE_DIM, LANES), np.float32)
    place[np.arange(MLA_ROPE_DIM), MLA_NOPE_DIM + np.arange(MLA_ROPE_DIM)] = 1.0
    og = o + MLA_ROPE_DIM
    row = lambda a: a.reshape(1, -1).astype(F32)
    return {
        "w_all": jnp.concatenate([w_in[:, :o], w_kr], axis=1).astype(BF16),
        "q_g": row(q_norm_g), "kv_g": row(kv_norm_g),
        "w_uq": _pad_heads(w_uq, MLA_NOPE_DIM + MLA_ROPE_DIM).astype(BF16),
        "w_uk": _pad_heads(w_uk, MLA_NOPE_DIM).astype(BF16),
        "w_uv": w_uv.astype(BF16),
        "place": jnp.asarray(place, dtype=BF16),
        "w_ga": w_in[:, og:og + D_MODEL].astype(BF16), "w_gb": w_in[:, og + D_MODEL:og + 2 * D_MODEL].astype(BF16),
        "w_oa": w_oa.astype(BF16), "w_ob": w_ob.astype(BF16), "w_out": w_out.astype(BF16),
        "ln1_g": row(ln1_g), "ln1_b": row(ln1_b), "ln2_g": row(ln2_g), "ln2_b": row(ln2_b),
        "wq_t": peer_wq.T.astype(BF16),
        "keys": peer_sub_keys.reshape(2 * PEER_HEADS, PEER_N_KEYS, PEER_HALF).astype(BF16),
        "u": peer_u.astype(BF16),
        "v_t": peer_v.T.astype(BF16),
    }


def _tile(n, pref):
    return pref if n % pref == 0 else n


def _layer(x, pos, past, w, alpha):
    b, t, _ = x.shape
    n = b * t
    tabs = _rope_tables(pos)
    if t % TOKEN_TILE == 0:
        tm = TOKEN_TILE
    elif TOKEN_TILE % t == 0 and n % TOKEN_TILE == 0:
        tm = TOKEN_TILE
        tabs = tuple(jnp.tile(a, (TOKEN_TILE // t, 1)) for a in tabs)
    else:
        tm = t
    xf = x.reshape(n, D_MODEL)
    sbq, sbk, sbv, sbkb, sbvb, ckv, kpe, qcat, kcat, vm = _proj(xf, tabs, w, tm)
    r3 = lambda a: a.reshape(b, t, a.shape[-1])
    q_sb, k_sb, v_sb, q_m, k_m, v_m = map(r3, (sbq, sbkb, sbvb, qcat, kcat, vm))
    if past is None:
        q_off, tk_valid = 0, t
    else:
        p_len = past[0].shape[1]
        flat = lambda a: a.reshape((b * p_len,) + a.shape[2:])
        pkc, pvm = _cache_prep(flat(past[2]), flat(past[3]), w, _tile(b * p_len, CACHE_TILE))
        p3 = lambda a: a.reshape(b, p_len, a.shape[-1])
        k_sb = jnp.concatenate([past[0].astype(BF16).reshape(b, p_len, SB_WIDTH), k_sb], axis=1)
        v_sb = jnp.concatenate([past[1].astype(BF16).reshape(b, p_len, SB_WIDTH), v_sb], axis=1)
        k_m = jnp.concatenate([p3(pkc), k_m], axis=1)
        v_m = jnp.concatenate([p3(pvm), v_m], axis=1)
        q_off, tk_valid = p_len, p_len + t
    tq = _tile(t, Q_TILE)
    pad_keys = lambda a, tk: jnp.pad(a, ((0, 0), (0, (-a.shape[1]) % tk), (0, 0)))
    o_a = _sb_attention(q_sb, pad_keys(k_sb, K_TILE), pad_keys(v_sb, K_TILE), q_off, tq, K_TILE)
    o_b = _mla_attention(q_m, pad_keys(k_m, MLA_K_TILE), pad_keys(v_m, MLA_K_TILE), q_off, tk_valid, tq, MLA_K_TILE)
    h = _merge(xf, o_a.reshape(n, SB_WIDTH), o_b.reshape(n, MLA_WIDTH), w, tm, alpha)
    y = _peer(h, w, _tile(n, TOKEN_TILE), alpha)
    state = (sbk.reshape(b, t, SB_HEADS, SB_HEAD_DIM), sbv.reshape(b, t, SB_HEADS, SB_HEAD_DIM),
             ckv.reshape(b, t, MLA_KV_LORA), kpe.reshape(b, t, MLA_ROPE_DIM))
    return y.reshape(b, t, D_MODEL), state


def kernel(x_prompt, x_sample, cache_sb_k, cache_sb_v, cache_mla_ckv, cache_mla_kpe, w_in, q_norm_g, w_uq,
           kv_norm_g, w_uk, w_uv, w_oa, w_ob, w_out, ln1_g, ln1_b, peer_wq, peer_sub_keys, peer_u, peer_v,
           ln2_g, ln2_b):
    depth = w_in.shape[0]
    alpha = (2.0 * depth) ** 0.25
    past_len = cache_sb_k.shape[2]
    pos_p = jnp.arange(x_prompt.shape[1], dtype=jnp.int32)
    pos_s = past_len + jnp.arange(x_sample.shape[1], dtype=jnp.int32)
    y_p, y_s = x_prompt, x_sample
    st_p, st_s = [], []
    for l in range(depth):
        w = _prep_weights(w_in[l], q_norm_g[l], w_uq[l], kv_norm_g[l], w_uk[l], w_uv[l], w_oa[l], w_ob[l],
                          w_out[l], ln1_g[l], ln1_b[l], peer_wq[l], peer_sub_keys[l], peer_u[l], peer_v[l],
                          ln2_g[l], ln2_b[l])
        y_p, sp = _layer(y_p, pos_p, None, w, alpha)
        y_s, ss = _layer(y_s, pos_s, (cache_sb_k[l], cache_sb_v[l], cache_mla_ckv[l], cache_mla_kpe[l]), w, alpha)
        st_p.append(sp)
        st_s.append(ss)
    stack = lambda sts, i: jnp.stack([s[i] for s in sts])
    return (y_p, y_s, stack(st_p, 0), stack(st_p, 1), stack(st_p, 2), stack(st_p, 3),
            stack(st_s, 0), stack(st_s, 1), stack(st_s, 2), stack(st_s, 3))
```

```python
import functools

import numpy as np
import jax
import jax.numpy as jnp
from jax import lax
from jax.experimental import pallas as pl
from jax.experimental.pallas import tpu as pltpu

F32 = jnp.float32
BF16 = jnp.bfloat16

D_MODEL = 1024
CHUNK = 64
SB_HEADS = 8
SB_HEAD_DIM = 64
SB_WIDTH = SB_HEADS * SB_HEAD_DIM
MLA_HEADS = 8
MLA_NOPE_DIM = 64
MLA_ROPE_DIM = 32
MLA_V_DIM = 64
MLA_Q_LORA = 384
MLA_KV_LORA = 256
MLA_WIDTH = MLA_HEADS * MLA_V_DIM
ROPE_THETA = 10000.0
PEER_HEADS = 8
PEER_N_KEYS = 128
PEER_TOPK = 16
PEER_HALF = 128
EPS = 1e-5
NEG_INF = -1e30
LOG2_E = 1.4426950408889634
EXP_ZERO_F32 = -104.0

LANES = 128
HEAD_PAD = LANES
ROPE_HALF = MLA_ROPE_DIM // 2
VMEM_LIMIT = 56 * 2**20
TOKEN_TILE = 512
CACHE_TILE = 1024
Q_TILE = 512
K_TILE = 256
MLA_K_TILE = 512
MLA_Q_TILE = 1024


def _cparams(sem):
    return pltpu.CompilerParams(dimension_semantics=sem, vmem_limit_bytes=VMEM_LIMIT)


def _full(shape):
    nd = len(shape)
    return pl.BlockSpec(shape, lambda *_: (0,) * nd)


def _rms(x, g):
    ms = jnp.mean(x * x, axis=-1, keepdims=True)
    return x * lax.rsqrt(ms + EPS) * g


def _layer_norm(x, g, b):
    mu = jnp.mean(x, axis=-1, keepdims=True)
    xc = x - mu
    var = jnp.mean(xc * xc, axis=-1, keepdims=True)
    return xc * lax.rsqrt(var + EPS) * g + b


def _rope_lanes(x, c, s1, s2):
    return x * c + pltpu.roll(x, LANES - ROPE_HALF, 1) * s1 + pltpu.roll(x, ROPE_HALF, 1) * s2


W_ALL = 3 * SB_WIDTH + MLA_Q_LORA + MLA_KV_LORA + LANES


def _proj_kernel(x_ref, cq_ref, ck_ref, s1_ref, s2_ref, wall_ref, qg_ref, kvg_ref, wuq_ref, wuk_ref, wuv_ref,
                 sbq_o, sbk_o, sbv_o, sbkb_o, sbvb_o, ckv_o, kpe_o, qcat_o, kcat_o, vm_o):
    xb = x_ref[...].astype(BF16)
    proj = jnp.dot(xb, wall_ref[...], preferred_element_type=F32)
    sbq_o[...] = (proj[:, 0:SB_WIDTH] * (SB_HEAD_DIM ** -0.5)).astype(BF16)
    k = proj[:, SB_WIDTH:2 * SB_WIDTH]
    v = proj[:, 2 * SB_WIDTH:3 * SB_WIDTH]
    for h in range(SB_HEADS):
        sl = slice(h * SB_HEAD_DIM, (h + 1) * SB_HEAD_DIM)
        sbk_o[:, h, :] = k[:, sl]
        sbv_o[:, h, :] = v[:, sl]
    sbkb_o[...] = k.astype(BF16)
    sbvb_o[...] = v.astype(BF16)
    o = 3 * SB_WIDTH
    cqn = _rms(proj[:, o:o + MLA_Q_LORA], qg_ref[...])
    q = jnp.dot(cqn.astype(BF16), wuq_ref[...], preferred_element_type=F32)
    o += MLA_Q_LORA
    ckv = _rms(proj[:, o:o + MLA_KV_LORA], kvg_ref[...])
    ckv_o[...] = ckv
    o += MLA_KV_LORA
    cb = ckv.astype(BF16)
    kn = jnp.dot(cb, wuk_ref[...], preferred_element_type=F32)
    vm_o[...] = jnp.dot(cb, wuv_ref[...], preferred_element_type=F32).astype(BF16)
    s1 = s1_ref[...]
    s2 = s2_ref[...]
    kp = _rope_lanes(proj[:, o:o + LANES], ck_ref[...], s1, s2)
    kpe_o[...] = kp[:, MLA_NOPE_DIM:MLA_NOPE_DIM + MLA_ROPE_DIM]
    cq = cq_ref[...]
    for h in range(MLA_HEADS):
        sl = slice(h * HEAD_PAD, (h + 1) * HEAD_PAD)
        qcat_o[:, sl] = _rope_lanes(q[:, sl], cq, s1, s2).astype(BF16)
        kcat_o[:, sl] = (kn[:, sl] + kp).astype(BF16)


def _proj(x, tabs, w, tm):
    n = x.shape[0]
    rt = tabs[0].shape[0] // tm
    tok = lambda width: pl.BlockSpec((tm, width), lambda i: (i, 0))
    tab = pl.BlockSpec((tm, LANES), lambda i: (i % rt, 0))
    outs = [(SB_WIDTH, BF16), (SB_WIDTH, F32), (SB_WIDTH, F32), (SB_WIDTH, BF16), (SB_WIDTH, BF16),
            (MLA_KV_LORA, F32), (MLA_ROPE_DIM, F32), (MLA_HEADS * HEAD_PAD, BF16), (MLA_HEADS * HEAD_PAD, BF16),
            (MLA_WIDTH, BF16)]
    weights = (w["w_all"], w["q_g"], w["kv_g"], w["w_uq"], w["w_uk"], w["w_uv"])
    out_specs = [tok(wd) for wd, _ in outs]
    out_shape = [jax.ShapeDtypeStruct((n, wd), dt) for wd, dt in outs]
    for i in (1, 2):
        out_specs[i] = pl.BlockSpec((tm, SB_HEADS, SB_HEAD_DIM), lambda i: (i, 0, 0))
        out_shape[i] = jax.ShapeDtypeStruct((n, SB_HEADS, SB_HEAD_DIM), F32)
    return pl.pallas_call(
        _proj_kernel,
        grid=(n // tm,),
        in_specs=[tok(D_MODEL), tab, tab, tab, tab] + [_full(a.shape) for a in weights],
        out_specs=out_specs,
        out_shape=out_shape,
        compiler_params=_cparams(("parallel",)),
        name="proj",
    )(x, *tabs, *weights)


def _cache_kernel(ckv_ref, kpe_ref, wuk_ref, wuv_ref, place_ref, kcat_o, vm_o):
    cb = ckv_ref[...].astype(BF16)
    kn = jnp.dot(cb, wuk_ref[...], preferred_element_type=F32)
    vm_o[...] = jnp.dot(cb, wuv_ref[...], preferred_element_type=F32).astype(BF16)
    kp = jnp.dot(kpe_ref[...].astype(BF16), place_ref[...], preferred_element_type=F32)
    for h in range(MLA_HEADS):
        sl = slice(h * HEAD_PAD, (h + 1) * HEAD_PAD)
        kcat_o[:, sl] = (kn[:, sl] + kp).astype(BF16)


def _cache_prep(ckv, kpe, w, tm):
    n = ckv.shape[0]
    tok = lambda width: pl.BlockSpec((tm, width), lambda i: (i, 0))
    outs = [(MLA_HEADS * HEAD_PAD, BF16), (MLA_WIDTH, BF16)]
    weights = (w["w_uk"], w["w_uv"], w["place"])
    return pl.pallas_call(
        _cache_kernel,
        grid=(n // tm,),
        in_specs=[tok(MLA_KV_LORA), tok(MLA_ROPE_DIM)] + [_full(a.shape) for a in weights],
        out_specs=[tok(wd) for wd, _ in outs],
        out_shape=[jax.ShapeDtypeStruct((n, wd), dt) for wd, dt in outs],
        compiler_params=_cparams(("parallel",)),
        name="cache_prep",
    )(ckv, kpe, *weights)


def _loop(n, step):
    def one(i, carry):
        step(i)
        return carry

    lax.fori_loop(0, n, one, 0)


def _sb_kernel(q_ref, k_ref, v_ref, tri_ref, o_ref, acc_ref, c_ref, *, tq, tk, nk, q_off):
    qi = pl.program_id(2)
    q = q_ref[0]
    lane = lax.broadcasted_iota(jnp.int32, (tq, LANES), 1)
    zero = jnp.zeros_like(q)
    qs = (jnp.where(lane < SB_HEAD_DIM, q, zero), jnp.where(lane >= SB_HEAD_DIM, q, zero))
    q_lo = q_off + qi * tq
    nkb = jnp.minimum(nk, (q_lo + tq - 1 + tk - 1) // tk)
    kb_diag = jnp.minimum(q_lo // tk, nkb)
    acc_ref[...] = jnp.zeros_like(acc_ref)
    c_ref[...] = jnp.zeros_like(c_ref)
    q_pos = lax.broadcasted_iota(jnp.int32, (tk, tq), 1) + q_lo
    krow = lax.broadcasted_iota(jnp.int32, (tk, tq), 0)
    tri = tri_ref[...]

    def blocks(kbs, masked):
        k0s = [pl.multiple_of(kb * tk, tk) for kb in kbs]
        zs = [[lax.dot_general(k_ref[0, pl.ds(k0, tk), :], qs[h], (((1,), (1,)), ((), ())),
                               preferred_element_type=F32) for h in range(2)] for k0 in k0s]
        stage = []
        for k0, z2 in zip(k0s, zs):
            causal = (krow + k0) < q_pos if masked else None
            for z in z2:
                lb = jnp.minimum(z, 0.0) - jnp.log(1.0 + jnp.exp(-jnp.abs(z)))
                lom = lb - z
                if masked:
                    lom = jnp.where(causal, lom, 0.0)
                hi = lom.astype(BF16)
                lo = (lom - hi.astype(F32)).astype(BF16)
                suf = jnp.dot(tri, jnp.concatenate([hi, lo], axis=0), preferred_element_type=F32)
                stage.append((lb, suf, suf[0:1, :] + lom[0:1, :], causal))
        for b, k0 in enumerate(k0s):
            v = v_ref[0, pl.ds(k0, tk), :]
            for h in range(2):
                lb, suf, tot, causal = stage[2 * b + h]
                c = c_ref[h, 0:1, :]
                a = jnp.exp(lb + suf + c)
                if masked:
                    a = jnp.where(causal, a, 0.0)
                acc_ref[h] += lax.dot_general(v, a.astype(BF16), (((0,), (0,)), ((), ())),
                                              preferred_element_type=F32)
                c_ref[h] = jnp.broadcast_to(c + tot, (8, tq))

    n_diag = nkb - kb_diag
    _loop(n_diag // 2, lambda i: blocks((nkb - 1 - 2 * i, nkb - 2 - 2 * i), True))
    _loop(n_diag % 2, lambda _: blocks((kb_diag,), True))

    def live():
        return (jnp.max(c_ref[...]) > EXP_ZERO_F32).astype(jnp.int32)

    def more(state):
        return jnp.logical_and(state[0] < kb_diag, state[1] > 0)

    def back(state):
        blocks((kb_diag - 1 - state[0],), False)
        return state[0] + 1, live()

    lax.while_loop(more, back, (jnp.int32(0), live()))
    sub = lax.broadcasted_iota(jnp.int32, (LANES, tq), 0)
    o_ref[0] = jnp.where(sub < SB_HEAD_DIM, acc_ref[0], acc_ref[1]).T.astype(BF16)


def _suffix_matrix(tk):
    s = np.arange(tk)[:, None]
    j = np.arange(tk)[None, :]
    tri = (j > s).astype(np.float32)
    return jnp.asarray(np.concatenate([tri, tri], axis=1), dtype=BF16)


def _sb_attention(q, k, v, q_off, tq, tk):
    b, t_q, _ = q.shape
    t_k = k.shape[1]
    nk = t_k // tk
    kern = functools.partial(_sb_kernel, tq=tq, tk=tk, nk=nk, q_off=q_off)
    kv_spec = pl.BlockSpec((1, t_k, LANES), lambda bi, p, qi: (bi, 0, p))
    return pl.pallas_call(
        kern,
        grid=(b, SB_WIDTH // LANES, t_q // tq),
        in_specs=[pl.BlockSpec((1, tq, LANES), lambda bi, p, qi: (bi, qi, p)), kv_spec, kv_spec,
                  _full((tk, 2 * tk))],
        out_specs=pl.BlockSpec((1, tq, LANES), lambda bi, p, qi: (bi, qi, p)),
        out_shape=jax.ShapeDtypeStruct((b, t_q, SB_WIDTH), BF16),
        scratch_shapes=[pltpu.VMEM((2, LANES, tq), F32), pltpu.VMEM((2, 8, tq), F32)],
        compiler_params=_cparams(("parallel", "parallel", "arbitrary")),
        name="sb_attention",
    )(q, k, v, _suffix_matrix(tk))


def _mla_kernel(qa_ref, qb_ref, ka_ref, kb_ref, v_ref, o_ref, acc_ref, m_ref, l_ref, *, tq, tk, nk, q_off, tk_valid):
    qi = pl.program_id(2)
    q_lo = q_off + qi * tq
    k_hi = ((q_lo + tq - 1) // CHUNK + 1) * CHUNK
    nkb = jnp.minimum(nk, (k_hi + tk - 1) // tk)
    n_free = jnp.minimum(jnp.minimum(((q_lo // CHUNK + 1) * CHUNK) // tk, tk_valid // tk), nkb)
    acc_ref[...] = jnp.zeros_like(acc_ref)
    l_ref[...] = jnp.zeros_like(l_ref)
    m_ref[...] = jnp.full_like(m_ref, NEG_INF)
    q_chunk = (lax.broadcasted_iota(jnp.int32, (tk, tq), 1) + q_lo) // CHUNK
    krow = lax.broadcasted_iota(jnp.int32, (tk, tq), 0)
    c2 = (MLA_NOPE_DIM + MLA_ROPE_DIM) ** -0.5 * LOG2_E
    qs = (qa_ref[0], qb_ref[0])
    k_refs = (ka_ref, kb_ref)

    def blocks(kbs, masked):
        k0s = [pl.multiple_of(kb * tk, tk) for kb in kbs]
        raws = [[lax.dot_general(k_refs[h][0, pl.ds(k0, tk), :], qs[h], (((1,), (1,)), ((), ())),
                                 preferred_element_type=F32) for h in range(2)] for k0 in k0s]
        for k0, raw2 in zip(k0s, raws):
            v = v_ref[0, pl.ds(k0, tk), :]
            if masked:
                kidx = krow + k0
                mask = jnp.where(kidx < tk_valid, kidx // CHUNK, q_chunk + 1) <= q_chunk
            for h in range(2):
                raw = jnp.where(mask, raw2[h], NEG_INF) if masked else raw2[h]
                m_prev = m_ref[h, 0:1, :]
                m_next = jnp.maximum(m_prev, jnp.max(raw, axis=0, keepdims=True))
                p = jnp.exp2((raw - m_next) * c2)
                alpha = jnp.exp2((m_prev - m_next) * c2)
                l_next = alpha * l_ref[h, 0:1, :] + jnp.sum(p, axis=0, keepdims=True)
                pv = lax.dot_general(v, p.astype(BF16), (((0,), (0,)), ((), ())), preferred_element_type=F32)
                acc_ref[h] = alpha * acc_ref[h] + pv
                l_ref[h] = jnp.broadcast_to(l_next, (8, tq))
                m_ref[h] = jnp.broadcast_to(m_next, (8, tq))

    n4 = n_free // 4
    _loop(n4, lambda i: blocks(tuple(4 * i + d for d in range(4)), False))
    _loop((n_free % 4) // 2, lambda _: blocks((4 * n4, 4 * n4 + 1), False))
    _loop(n_free % 2, lambda _: blocks((n_free - 1,), False))
    n_edge = nkb - n_free
    _loop(n_edge // 2, lambda i: blocks((n_free + 2 * i, n_free + 2 * i + 1), True))
    _loop(n_edge % 2, lambda _: blocks((nkb - 1,), True))
    sub = lax.broadcasted_iota(jnp.int32, (LANES, tq), 0)
    out_t = jnp.where(sub < MLA_V_DIM, acc_ref[0] / l_ref[0, 0:1, :], acc_ref[1] / l_ref[1, 0:1, :])
    o_ref[0] = out_t.T.astype(BF16)


def _mla_attention(q, k, v, q_off, tk_valid, tq, tk):
    b, t_q, _ = q.shape
    t_k = k.shape[1]
    nk = t_k // tk
    kern = functools.partial(_mla_kernel, tq=tq, tk=tk, nk=nk, q_off=q_off, tk_valid=tk_valid)
    q_spec = lambda o: pl.BlockSpec((1, tq, LANES), lambda bi, p, qi: (bi, qi, 2 * p + o))
    k_spec = lambda o: pl.BlockSpec((1, t_k, LANES), lambda bi, p, qi: (bi, 0, 2 * p + o))
    return pl.pallas_call(
        kern,
        grid=(b, MLA_HEADS // 2, t_q // tq),
        in_specs=[q_spec(0), q_spec(1), k_spec(0), k_spec(1),
                  pl.BlockSpec((1, t_k, LANES), lambda bi, p, qi: (bi, 0, p))],
        out_specs=pl.BlockSpec((1, tq, LANES), lambda bi, p, qi: (bi, qi, p)),
        out_shape=jax.ShapeDtypeStruct((b, t_q, MLA_WIDTH), BF16),
        scratch_shapes=[pltpu.VMEM((2, LANES, tq), F32), pltpu.VMEM((2, 8, tq), F32), pltpu.VMEM((2, 8, tq), F32)],
        compiler_params=_cparams(("parallel", "parallel", "arbitrary")),
        name="mla_attention",
    )(q, q, k, k, v)


def _merge_kernel(x_ref, oa_ref, ob_ref, wga_ref, wgb_ref, woa_ref, wob_ref, wout_ref, g_ref, b_ref, h_o, *, alpha):
    x = x_ref[...]
    xb = x.astype(BF16)
    ga = jax.nn.sigmoid(jnp.dot(xb, wga_ref[...], preferred_element_type=F32))
    gb = jax.nn.sigmoid(jnp.dot(xb, wgb_ref[...], preferred_element_type=F32))
    merged = (ga * jnp.dot(oa_ref[...], woa_ref[...], preferred_element_type=F32)
              + gb * jnp.dot(ob_ref[...], wob_ref[...], preferred_element_type=F32))
    y = alpha * x + jnp.dot(merged.astype(BF16), wout_ref[...], preferred_element_type=F32)
    h_o[...] = _layer_norm(y, g_ref[...], b_ref[...])


def _merge(x, oa, ob, w, tm, alpha):
    n = x.shape[0]
    tok = lambda width: pl.BlockSpec((tm, width), lambda i: (i, 0))
    weights = (w["w_ga"], w["w_gb"], w["w_oa"], w["w_ob"], w["w_out"], w["ln1_g"], w["ln1_b"])
    return pl.pallas_call(
        functools.partial(_merge_kernel, alpha=alpha),
        grid=(n // tm,),
        in_specs=[tok(D_MODEL), tok(SB_WIDTH), tok(MLA_WIDTH)] + [_full(a.shape) for a in weights],
        out_specs=tok(D_MODEL),
        out_shape=jax.ShapeDtypeStruct((n, D_MODEL), F32),
        compiler_params=_cparams(("parallel",)),
        name="merge",
    )(x, oa, ob, *weights)


PEER_TE = 1024
I1_PER_TILE = PEER_TE // PEER_N_KEYS
W_CHUNK = 512
CAND_COUNTS = tuple(PEER_TOPK // (a + 1) for a in range(PEER_TOPK))
CAND_STARTS = tuple(int(x) for x in np.cumsum((0,) + CAND_COUNTS[:-1]))
N_CAND = sum(CAND_COUNTS)
CAND_ROWS = -(-N_CAND // 8) * 8


def _extract_rounds(s_refs, n_rows, tb):
    row = lax.broadcasted_iota(jnp.int32, (n_rows, tb), 0).astype(F32)
    slot = lax.broadcasted_iota(jnp.int32, (PEER_TOPK, tb), 0)

    def body(r, carry):
        out = []
        for s_ref, (vals, firsts) in zip(s_refs, carry):
            s = s_ref[...]
            m = jnp.max(s, axis=0, keepdims=True)
            first = jnp.min(jnp.where(s == m, row, float(n_rows)), axis=0, keepdims=True)
            s_ref[...] = jnp.where(row == first, -jnp.inf, s)
            out.append((jnp.where(slot == r, m, vals), jnp.where(slot == r, first, firsts)))
        return tuple(out)

    init = tuple((jnp.zeros((PEER_TOPK, tb), F32), jnp.zeros((PEER_TOPK, tb), F32)) for _ in s_refs)
    return lax.fori_loop(0, PEER_TOPK, body, init)


def _remove_maxima(s_refs, tb):
    slot = lax.broadcasted_iota(jnp.int32, (PEER_TOPK, tb), 0)

    def body(r, carry):
        out = []
        for s_ref, vals in zip(s_refs, carry):
            s = s_ref[...]
            m = jnp.max(s, axis=0, keepdims=True)
            s_ref[...] = jnp.where(s == m, -jnp.inf, s)
            out.append(jnp.where(slot == r, m, vals))
        return tuple(out)

    return lax.fori_loop(0, PEER_TOPK, body, tuple(jnp.zeros((PEER_TOPK, tb), F32) for _ in s_refs))


def _removed_mismatch(s_after, expected):
    gone = jnp.where(s_after == -jnp.inf, 1.0, 0.0)
    per_col = jnp.sum(gone.reshape(-1, gone.shape[-1]), axis=0, keepdims=True)
    return (jnp.max(jnp.abs(per_col - float(expected))) > 0.0).astype(jnp.int32)


def _scatter_rows(firsts, values, n_rows, tb, fill):
    row = lax.broadcasted_iota(jnp.int32, (n_rows, tb), 0).astype(F32)
    out = jnp.full((n_rows, tb), fill, F32)
    for r in range(PEER_TOPK):
        out = jnp.where(row == firsts[r:r + 1, :], values[r], out)
    return out


def _peer_kernel(h_ref, wqt_ref, keys_ref, u_ref, vt_ref, g_ref, b_ref, y_o,
                 htb_ref, qt_ref, n1_ref, e1_ref, rank2_ref, e2_ref, work_ref, cand_ref, a0_ref, a1_ref, outt_ref,
                 *, tb, alpha):
    j = pl.program_id(1)

    @pl.when(j == 0)
    def _prologue():
        htb = h_ref[...].T.astype(BF16)
        htb_ref[...] = htb
        qt_ref[...] = jnp.dot(wqt_ref[...], htb, preferred_element_type=F32)
        outt_ref[...] = jnp.zeros_like(outt_ref)
        a1_ref[...] = jnp.zeros_like(a1_ref)

        def fill_candidates(t1, t2):
            cand_ref[...] = jnp.full_like(cand_ref, -jnp.inf)
            for a in range(PEER_TOPK):
                cand_ref[CAND_STARTS[a]:CAND_STARTS[a] + CAND_COUNTS[a], :] = t1[a:a + 1, :] + t2[0:CAND_COUNTS[a], :]

        def finish(h, s, t1, t2, top, n1, rank2):
            n1_ref[h] = n1
            rank2_ref[h] = rank2.astype(BF16)
            z = jnp.sum(jnp.exp(top - top[0:1, :]), axis=0, keepdims=True)
            e1_ref[h] = jnp.exp(s[0] - t1[0:1, :])
            e2_ref[h] = (jnp.exp(s[1] - t2[0:1, :]) / z).astype(BF16)

        def head_by_value(h, s):
            work_ref[0] = s[0]
            work_ref[1] = s[1]
            t1, t2 = _remove_maxima((work_ref.at[0], work_ref.at[1]), tb)
            bad = _removed_mismatch(work_ref[...], 2 * PEER_TOPK)
            fill_candidates(t1, t2)
            (top,) = _remove_maxima((cand_ref,), tb)
            gone = cand_ref[...] == -jnp.inf
            bad = jnp.maximum(bad, _removed_mismatch(cand_ref[...], PEER_TOPK + CAND_ROWS - N_CAND))
            n1 = jnp.zeros((PEER_N_KEYS, tb), F32)
            rank2 = jnp.full((PEER_N_KEYS, tb), float(PEER_TOPK), F32)
            for a in range(PEER_TOPK):
                cnt = jnp.sum(jnp.where(gone[CAND_STARTS[a]:CAND_STARTS[a] + CAND_COUNTS[a], :], 1.0, 0.0),
                              axis=0, keepdims=True)
                n1 = jnp.where(s[0] == t1[a:a + 1, :], cnt, n1)
                rank2 = jnp.where(s[1] == t2[a:a + 1, :], float(a), rank2)
            finish(h, s, t1, t2, top, n1, rank2)
            return bad

        def head_exact(h, s):
            work_ref[0] = s[0]
            work_ref[1] = s[1]
            (t1, first1), (t2, first2) = _extract_rounds((work_ref.at[0], work_ref.at[1]), PEER_N_KEYS, tb)
            fill_candidates(t1, t2)
            ((top, cfirst),) = _extract_rounds((cand_ref,), CAND_ROWS, tb)
            counts = [jnp.sum(jnp.where(cfirst >= float(CAND_STARTS[a]),
                                        jnp.where(cfirst < float(CAND_STARTS[a] + CAND_COUNTS[a]), 1.0, 0.0), 0.0),
                              axis=0, keepdims=True) for a in range(PEER_TOPK)]
            n1 = _scatter_rows(first1, counts, PEER_N_KEYS, tb, 0.0)
            rank2 = _scatter_rows(first2, [float(r) for r in range(PEER_TOPK)], PEER_N_KEYS, tb, float(PEER_TOPK))
            finish(h, s, t1, t2, top, n1, rank2)

        def per_head(h, carry):
            s = []
            for c in range(2):
                r0 = pl.multiple_of((2 * h + c) * PEER_HALF, PEER_HALF)
                q = qt_ref[pl.ds(r0, PEER_HALF), :].astype(BF16)
                s.append(jnp.dot(keys_ref[2 * h + c], q, preferred_element_type=F32))
            tie = head_by_value(h, s)

            @pl.when(tie > 0)
            def _():
                head_exact(h, s)

            return carry

        lax.fori_loop(0, PEER_HEADS, per_head, 0)

    i1_0 = pl.multiple_of(jnp.maximum(j - 1, 0) * I1_PER_TILE, I1_PER_TILE)

    def step(a_next_ref, a_prev_ref):
        a_next_ref[...] = jnp.dot(u_ref[...], htb_ref[...], preferred_element_type=F32)
        n1_rows = [n1_ref[h, pl.ds(i1_0, I1_PER_TILE), :] for h in range(PEER_HEADS)]
        e1_rows = [e1_ref[h, pl.ds(i1_0, I1_PER_TILE), :] for h in range(PEER_HEADS)]
        for c in range(PEER_TE // W_CHUNK):
            w = []
            for i in range(c * W_CHUNK // PEER_N_KEYS, (c + 1) * W_CHUNK // PEER_N_KEYS):
                a = a_prev_ref[i * PEER_N_KEYS:(i + 1) * PEER_N_KEYS, :]
                act = 0.5 * a * (1.0 + lax.erf(a * (0.5 ** 0.5)))
                gate = jnp.zeros((PEER_N_KEYS, tb), BF16)
                for h in range(PEER_HEADS):
                    n1b = jnp.broadcast_to(n1_rows[h][i:i + 1, :], (PEER_N_KEYS, tb)).astype(BF16)
                    e1b = jnp.broadcast_to(e1_rows[h][i:i + 1, :], (PEER_N_KEYS, tb)).astype(BF16)
                    gate = gate + jnp.where(rank2_ref[h] < n1b, e1b * e2_ref[h], jnp.zeros_like(gate))
                w.append(gate * act.astype(BF16))
            outt_ref[...] += jnp.dot(vt_ref[:, c * W_CHUNK:(c + 1) * W_CHUNK], jnp.concatenate(w, axis=0),
                                     preferred_element_type=F32)

    @pl.when(j % 2 == 0)
    def _even():
        step(a0_ref, a1_ref)

    @pl.when(j % 2 == 1)
    def _odd():
        step(a1_ref, a0_ref)

    @pl.when(j == pl.num_programs(1) - 1)
    def _epilogue():
        h = h_ref[...]
        y_o[...] = _layer_norm(alpha * h + outt_ref[...].T, g_ref[...], b_ref[...])


def _peer(h, w, tb, alpha):
    n = h.shape[0]
    n_tiles = w["u"].shape[0] // PEER_TE
    weights_pre = (w["wq_t"], w["keys"])
    return pl.pallas_call(
        functools.partial(_peer_kernel, tb=tb, alpha=alpha),
        grid=(n // tb, n_tiles + 1),
        in_specs=[pl.BlockSpec((tb, D_MODEL), lambda i, j: (i, 0))]
        + [_full(a.shape) for a in weights_pre]
        + [pl.BlockSpec((PEER_TE, D_MODEL), lambda i, j: (jnp.minimum(j, n_tiles - 1), 0)),
           pl.BlockSpec((D_MODEL, PEER_TE), lambda i, j: (0, jnp.maximum(j - 1, 0))),
           _full(w["ln2_g"].shape), _full(w["ln2_b"].shape)],
        out_specs=pl.BlockSpec((tb, D_MODEL), lambda i, j: (i, 0)),
        out_shape=jax.ShapeDtypeStruct((n, D_MODEL), F32),
        scratch_shapes=[
            pltpu.VMEM((D_MODEL, tb), BF16),
            pltpu.VMEM((2 * PEER_HEADS * PEER_HALF, tb), F32),
            pltpu.VMEM((PEER_HEADS, PEER_N_KEYS, tb), F32),
            pltpu.VMEM((PEER_HEADS, PEER_N_KEYS, tb), F32),
            pltpu.VMEM((PEER_HEADS, PEER_N_KEYS, tb), BF16),
            pltpu.VMEM((PEER_HEADS, PEER_N_KEYS, tb), BF16),
            pltpu.VMEM((2, PEER_N_KEYS, tb), F32),
            pltpu.VMEM((CAND_ROWS, tb), F32),
            pltpu.VMEM((PEER_TE, tb), F32),
            pltpu.VMEM((PEER_TE, tb), F32),
            pltpu.VMEM((D_MODEL, tb), F32),
        ],
        compiler_params=_cparams(("parallel", "arbitrary")),
        name="peer",
    )(h, *weights_pre, w["u"], w["v_t"], w["ln2_g"], w["ln2_b"])


def _rope_tables(pos):
    inv = ROPE_THETA ** (-jnp.arange(ROPE_HALF, dtype=F32) / ROPE_HALF)
    ang = pos.astype(F32)[:, None] * inv[None, :]
    cos, sin = jnp.cos(ang), jnp.sin(ang)
    n = pos.shape[0]
    z16 = jnp.zeros((n, ROPE_HALF), F32)
    head = lambda fill: jnp.full((n, MLA_NOPE_DIM), fill, F32)
    tail = jnp.zeros((n, LANES - MLA_NOPE_DIM - MLA_ROPE_DIM), F32)
    cq = jnp.concatenate([head(1.0), cos, cos, tail], axis=1)
    ck = jnp.concatenate([head(0.0), cos, cos, tail], axis=1)
    s1 = jnp.concatenate([head(0.0), -sin, z16, tail], axis=1)
    s2 = jnp.concatenate([head(0.0), z16, sin, tail], axis=1)
    return cq, ck, s1, s2


def _pad_heads(wm, width):
    k = wm.shape[0]
    wm = wm.reshape(k, MLA_HEADS, width)
    return jnp.pad(wm, ((0, 0), (0, 0), (0, HEAD_PAD - width))).reshape(k, MLA_HEADS * HEAD_PAD)


def _prep_weights(w_in, q_norm_g, w_uq, kv_norm_g, w_uk, w_uv, w_oa, w_ob, w_out, ln1_g, ln1_b,
                  peer_wq, peer_sub_keys, peer_u, peer_v, ln2_g, ln2_b):
    o = 3 * SB_WIDTH + MLA_Q_LORA + MLA_KV_LORA
    w_kr = jnp.pad(w_in[:, o:o + MLA_ROPE_DIM], ((0, 0), (MLA_NOPE_DIM, LANES - MLA_NOPE_DIM - MLA_ROPE_DIM)))
    place = np.zeros((MLA_ROPE_DIM, LANES), np.float32)
    place[np.arange(MLA_ROPE_DIM), MLA_NOPE_DIM + np.arange(MLA_ROPE_DIM)] = 1.0
    og = o + MLA_ROPE_DIM
    row = lambda a: a.reshape(1, -1).astype(F32)
    return {
        "w_all": jnp.concatenate([w_in[:, :o], w_kr], axis=1).astype(BF16),
        "q_g": row(q_norm_g), "kv_g": row(kv_norm_g),
        "w_uq": _pad_heads(w_uq, MLA_NOPE_DIM + MLA_ROPE_DIM).astype(BF16),
        "w_uk": _pad_heads(w_uk, MLA_NOPE_DIM).astype(BF16),
        "w_uv": w_uv.astype(BF16),
        "place": jnp.asarray(place, dtype=BF16),
        "w_ga": w_in[:, og:og + D_MODEL].astype(BF16), "w_gb": w_in[:, og + D_MODEL:og + 2 * D_MODEL].astype(BF16),
        "w_oa": w_oa.astype(BF16), "w_ob": w_ob.astype(BF16), "w_out": w_out.astype(BF16),
        "ln1_g": row(ln1_g), "ln1_b": row(ln1_b), "ln2_g": row(ln2_g), "ln2_b": row(ln2_b),
        "wq_t": peer_wq.T.astype(BF16),
        "keys": peer_sub_keys.reshape(2 * PEER_HEADS, PEER_N_KEYS, PEER_HALF).astype(BF16),
        "u": peer_u.astype(BF16),
        "v_t": peer_v.T.astype(BF16),
    }


def _tile(n, pref):
    return pref if n % pref == 0 else n


def _layer(x, pos, past, w, alpha):
    b, t, _ = x.shape
    n = b * t
    tabs = _rope_tables(pos)
    if t % TOKEN_TILE == 0:
        tm = TOKEN_TILE
    elif TOKEN_TILE % t == 0 and n % TOKEN_TILE == 0:
        tm = TOKEN_TILE
        tabs = tuple(jnp.tile(a, (TOKEN_TILE // t, 1)) for a in tabs)
    else:
        tm = t
    xf = x.reshape(n, D_MODEL)
    sbq, sbk, sbv, sbkb, sbvb, ckv, kpe, qcat, kcat, vm = _proj(xf, tabs, w, tm)
    r3 = lambda a: a.reshape(b, t, a.shape[-1])
    q_sb, k_sb, v_sb, q_m, k_m, v_m = map(r3, (sbq, sbkb, sbvb, qcat, kcat, vm))
    if past is None:
        q_off, tk_valid = 0, t
    else:
        p_len = past[0].shape[1]
        flat = lambda a: a.reshape((b * p_len,) + a.shape[2:])
        pkc, pvm = _cache_prep(flat(past[2]), flat(past[3]), w, _tile(b * p_len, CACHE_TILE))
        p3 = lambda a: a.reshape(b, p_len, a.shape[-1])
        k_sb = jnp.concatenate([past[0].astype(BF16).reshape(b, p_len, SB_WIDTH), k_sb], axis=1)
        v_sb = jnp.concatenate([past[1].astype(BF16).reshape(b, p_len, SB_WIDTH), v_sb], axis=1)
        k_m = jnp.concatenate([p3(pkc), k_m], axis=1)
        v_m = jnp.concatenate([p3(pvm), v_m], axis=1)
        q_off, tk_valid = p_len, p_len + t
    tq = _tile(t, Q_TILE)
    pad_keys = lambda a, tk: jnp.pad(a, ((0, 0), (0, (-a.shape[1]) % tk), (0, 0)))
    o_a = _sb_attention(q_sb, pad_keys(k_sb, K_TILE), pad_keys(v_sb, K_TILE), q_off, tq, K_TILE)
    o_b = _mla_attention(q_m, pad_keys(k_m, MLA_K_TILE), pad_keys(v_m, MLA_K_TILE), q_off, tk_valid,
                         _tile(t, MLA_Q_TILE), MLA_K_TILE)
    h = _merge(xf, o_a.reshape(n, SB_WIDTH), o_b.reshape(n, MLA_WIDTH), w, tm, alpha)
    y = _peer(h, w, _tile(n, TOKEN_TILE), alpha)
    state = (sbk.reshape(b, t, SB_HEADS, SB_HEAD_DIM), sbv.reshape(b, t, SB_HEADS, SB_HEAD_DIM),
             ckv.reshape(b, t, MLA_KV_LORA), kpe.reshape(b, t, MLA_ROPE_DIM))
    return y.reshape(b, t, D_MODEL), state


def kernel(x_prompt, x_sample, cache_sb_k, cache_sb_v, cache_mla_ckv, cache_mla_kpe, w_in, q_norm_g, w_uq,
           kv_norm_g, w_uk, w_uv, w_oa, w_ob, w_out, ln1_g, ln1_b, peer_wq, peer_sub_keys, peer_u, peer_v,
           ln2_g, ln2_b):
    depth = w_in.shape[0]
    alpha = (2.0 * depth) ** 0.25
    past_len = cache_sb_k.shape[2]
    pos_p = jnp.arange(x_prompt.shape[1], dtype=jnp.int32)
    pos_s = past_len + jnp.arange(x_sample.shape[1], dtype=jnp.int32)
    y_p, y_s = x_prompt, x_sample
    st_p, st_s = [], []
    for l in range(depth):
        w = _prep_weights(w_in[l], q_norm_g[l], w_uq[l], kv_norm_g[l], w_uk[l], w_uv[l], w_oa[l], w_ob[l],
                          w_out[l], ln1_g[l], ln1_b[l], peer_wq[l], peer_sub_keys[l], peer_u[l], peer_v[l],
                          ln2_g[l], ln2_b[l])
        y_p, sp = _layer(y_p, pos_p, None, w, alpha)
        y_s, ss = _layer(y_s, pos_s, (cache_sb_k[l], cache_sb_v[l], cache_mla_ckv[l], cache_mla_kpe[l]), w, alpha)
        st_p.append(sp)
        st_s.append(ss)
    stack = lambda sts, i: jnp.stack([s[i] for s in sts])
    return (y_p, y_s, stack(st_p, 0), stack(st_p, 1), stack(st_p, 2), stack(st_p, 3),
            stack(st_s, 0), stack(st_s, 1), stack(st_s, 2), stack(st_s, 3))
```
